```python
import math
import jax, jax.numpy as jnp
from jax import lax
import numpy as np

D_MODEL = 2048
BATCH = 2
SEQ = 4096
DEPTH = 4

CHUNK = 64
N_MIXERS = 2
S5_WIDTH = D_MODEL
S5_GROUP = 16
S5_GROUPS = S5_WIDTH // S5_GROUP
S5_STATE = 64
DT_MIN = 0.001
DT_MAX = 0.1
EIG_CLIP = -1e-4
SGU_EXPAND = 6
SGU_HALF = SGU_EXPAND * D_MODEL // 2
SGU_BLOCK = 128
SGU_HEADS = 16
SGU_HEAD_DIM = SGU_HALF // SGU_HEADS
FFN_HIDDEN = 4 * D_MODEL
EPS = 1e-6

kernel_name = "hybrid_s5_sgu_streaming_encoder"


def rms_norm(x, g):
    xf = x.astype(jnp.float32)
    y = xf * lax.rsqrt(jnp.mean(xf * xf, axis=-1, keepdims=True) + EPS)
    return (y * g.astype(jnp.float32)).astype(x.dtype)


def layer_norm(x, g, b):
    xf = x.astype(jnp.float32)
    mu = jnp.mean(xf, axis=-1, keepdims=True)
    xc = xf - mu
    y = xc * lax.rsqrt(jnp.mean(xc * xc, axis=-1, keepdims=True) + EPS)
    return (y * g.astype(jnp.float32) + b.astype(jnp.float32)).astype(x.dtype)


def _linear_recurrence_op(e1, e2):
    a1r, a1i, b1r, b1i = e1
    a2r, a2i, b2r, b2i = e2
    return (a2r * a1r - a2i * a1i,
            a2r * a1i + a2i * a1r,
            a2r * b1r - a2i * b1i + b2r,
            a2r * b1i + a2i * b1r + b2i)


def s5_mixer(h, w_in, a_re, a_im, log_dt, b_re, b_im, c_re, c_im, d_skip, w_glu, w_out):
    f32 = jnp.float32
    bsz, seq, _ = h.shape
    u = (h @ w_in).astype(f32)
    u_g = u.reshape(bsz, seq, S5_GROUPS, S5_GROUP)
    lam_re = jnp.minimum(a_re.astype(f32), EIG_CLIP)
    lam_im = a_im.astype(f32)
    dt = jnp.exp(log_dt.astype(f32))[:, None]
    mag = jnp.exp(lam_re * dt)
    ab_re = mag * jnp.cos(lam_im * dt)
    ab_im = mag * jnp.sin(lam_im * dt)
    denom = lam_re * lam_re + lam_im * lam_im
    coef_re = ((ab_re - 1.0) * lam_re + ab_im * lam_im) / denom
    coef_im = (ab_im * lam_re - (ab_re - 1.0) * lam_im) / denom
    br = b_re.astype(f32)
    bi = b_im.astype(f32)
    bb_re = coef_re[..., None] * br - coef_im[..., None] * bi
    bb_im = coef_re[..., None] * bi + coef_im[..., None] * br
    bu_re = jnp.einsum('blgc,gpc->blgp', u_g, bb_re)
    bu_im = jnp.einsum('blgc,gpc->blgp', u_g, bb_im)
    a_seq_re = jnp.broadcast_to(ab_re, bu_re.shape)
    a_seq_im = jnp.broadcast_to(ab_im, bu_im.shape)
    _, _, s_re, s_im = lax.associative_scan(
        _linear_recurrence_op, (a_seq_re, a_seq_im, bu_re, bu_im), axis=1)
    y = (jnp.einsum('blgp,gcp->blgc', s_re, c_re.astype(f32))
         - jnp.einsum('blgp,gcp->blgc', s_im, c_im.astype(f32)))
    y = y.reshape(bsz, seq, S5_WIDTH) + d_skip.astype(f32) * u
    y = y.astype(h.dtype)
    z = jax.nn.gelu(y)
    z = z * jax.nn.sigmoid(z @ w_glu)
    return z @ w_out


def sgu_mixer(h, w_in, ln_g, ln_b, w_s, b_s, w_out):
    bsz, seq, _ = h.shape
    z = jax.nn.gelu(h @ w_in)
    u, v = jnp.split(z, 2, axis=-1)
    v = layer_norm(v, ln_g, ln_b)
    n_blk = seq // SGU_BLOCK
    v = v.reshape(bsz, n_blk, SGU_BLOCK, SGU_HEADS, SGU_HEAD_DIM)
    pos = jnp.arange(SGU_BLOCK)
    mask = (pos[None, :] // CHUNK) <= (pos[:, None] // CHUNK)
    w = jnp.where(mask[None], w_s, jnp.zeros((), w_s.dtype))
    s = jnp.einsum('hts,bnshd->bnthd', w, v) + b_s.T[None, None, :, :, None]
    s = s.reshape(bsz, seq, SGU_HALF)
    return (u * s) @ w_out


def squared_relu_mlp(h, w_up, w_down):
    a = jax.nn.relu(h @ w_up)
    return (a * a) @ w_down


def setup_inputs(seed: int = 0) -> dict:
    key = jax.random.key(seed)
    ks = jax.random.split(key, 24)
    n_a = (DEPTH + 1) // 2
    n_b = DEPTH // 2
    nrm = jax.random.normal
    f32 = jnp.float32
    x = nrm(ks[0], (BATCH, SEQ, D_MODEL), f32)
    norm_g = 1.0 + 0.02 * nrm(ks[1], (DEPTH, 4, D_MODEL), f32)
    s5_w_in = nrm(ks[2], (n_a, D_MODEL, S5_WIDTH), f32) * D_MODEL ** -0.5
    s5_a_re = -0.5 + 0.01 * nrm(ks[3], (n_a, S5_GROUPS, S5_STATE), f32)
    s5_a_im = (jnp.pi * jnp.arange(S5_STATE, dtype=f32))[None, None, :] + 0.01 * nrm(ks[4], (n_a, S5_GROUPS, S5_STATE), f32)
    s5_log_dt = jax.random.uniform(ks[5], (n_a, S5_GROUPS), f32, minval=math.log(DT_MIN), maxval=math.log(DT_MAX))
    b_scale = (2.0 * S5_GROUP) ** -0.5
    c_scale = (2.0 * S5_STATE) ** -0.5
    s5_b_re = nrm(ks[6], (n_a, S5_GROUPS, S5_STATE, S5_GROUP), f32) * b_scale
    s5_b_im = nrm(ks[7], (n_a, S5_GROUPS, S5_STATE, S5_GROUP), f32) * b_scale
    s5_c_re = nrm(ks[8], (n_a, S5_GROUPS, S5_GROUP, S5_STATE), f32) * c_scale
    s5_c_im = nrm(ks[9], (n_a, S5_GROUPS, S5_GROUP, S5_STATE), f32) * c_scale
    s5_d = nrm(ks[10], (n_a, S5_WIDTH), f32)
    s5_w_glu = nrm(ks[11], (n_a, S5_WIDTH, S5_WIDTH), f32) * S5_WIDTH ** -0.5
    s5_w_out = nrm(ks[12], (n_a, S5_WIDTH, D_MODEL), f32) * S5_WIDTH ** -0.5
    sgu_w_in = nrm(ks[13], (n_b, D_MODEL, 2 * SGU_HALF), f32) * D_MODEL ** -0.5
    sgu_ln_g = 1.0 + 0.02 * nrm(ks[14], (n_b, SGU_HALF), f32)
    sgu_ln_b = 0.02 * nrm(ks[15], (n_b, SGU_HALF), f32)
    sgu_w_s = nrm(ks[16], (n_b, SGU_HEADS, SGU_BLOCK, SGU_BLOCK), f32) * (0.5 * SGU_BLOCK ** -0.5)
    sgu_b_s = 1.0 + 0.1 * nrm(ks[17], (n_b, SGU_HEADS, SGU_BLOCK), f32)
    sgu_w_out = nrm(ks[18], (n_b, SGU_HALF, D_MODEL), f32) * SGU_HALF ** -0.5
    ffn_w_up = nrm(ks[19], (DEPTH, D_MODEL, FFN_HIDDEN), f32) * D_MODEL ** -0.5
    ffn_w_down = nrm(ks[20], (DEPTH, FFN_HIDDEN, D_MODEL), f32) * FFN_HIDDEN ** -0.5
    return {"x": x, "norm_g": norm_g,
            "s5_w_in": s5_w_in, "s5_a_re": s5_a_re, "s5_a_im": s5_a_im, "s5_log_dt": s5_log_dt,
            "s5_b_re": s5_b_re, "s5_b_im": s5_b_im, "s5_c_re": s5_c_re, "s5_c_im": s5_c_im,
            "s5_d": s5_d, "s5_w_glu": s5_w_glu, "s5_w_out": s5_w_out,
            "sgu_w_in": sgu_w_in, "sgu_ln_g": sgu_ln_g, "sgu_ln_b": sgu_ln_b,
            "sgu_w_s": sgu_w_s, "sgu_b_s": sgu_b_s, "sgu_w_out": sgu_w_out,
            "ffn_w_up": ffn_w_up, "ffn_w_down": ffn_w_down}


def reference(x, norm_g, s5_w_in, s5_a_re, s5_a_im, s5_log_dt, s5_b_re, s5_b_im, s5_c_re, s5_c_im,
              s5_d, s5_w_glu, s5_w_out, sgu_w_in, sgu_ln_g, sgu_ln_b, sgu_w_s, sgu_b_s, sgu_w_out,
              ffn_w_up, ffn_w_down):
    h = x
    for i in range(DEPTH):
        g = norm_g[i]
        hn = rms_norm(h, g[0])
        j = i // N_MIXERS
        if i % N_MIXERS == 0:
            m = s5_mixer(hn, s5_w_in[j], s5_a_re[j], s5_a_im[j], s5_log_dt[j], s5_b_re[j], s5_b_im[j],
                         s5_c_re[j], s5_c_im[j], s5_d[j], s5_w_glu[j], s5_w_out[j])
        else:
            m = sgu_mixer(hn, sgu_w_in[j], sgu_ln_g[j], sgu_ln_b[j], sgu_w_s[j], sgu_b_s[j], sgu_w_out[j])
        h = h + rms_norm(m, g[1])
        f = squared_relu_mlp(rms_norm(h, g[2]), ffn_w_up[i], ffn_w_down[i])
        h = h + rms_norm(f, g[3])
    return h
```

```python
import functools

import jax
import jax.numpy as jnp
from jax import lax
from jax.experimental import pallas as pl
from jax.experimental.pallas import tpu as pltpu

F32 = jnp.float32
BF16 = jnp.bfloat16

EPS = 1e-6
EIG_CLIP = -1e-4
SEQ_CHUNK = 64
SGU_BLOCK = 128
S5_GROUP = 16
SSM_CHUNK = 8
LANES = 128
GROUPS_PER_TILE = LANES // S5_GROUP
VMEM_LIMIT = 56 * 1024 * 1024


def _params(*sem):
    return pltpu.CompilerParams(dimension_semantics=sem, vmem_limit_bytes=VMEM_LIMIT)


def _rms(x, g):
    ms = jnp.mean(x * x, axis=-1, keepdims=True)
    return x * lax.rsqrt(ms + EPS) * g


def _relu2(y):
    a = jnp.maximum(y, 0.0)
    return a * a


def _up_kernel(x_ref, g_ref, w_ref, o_ref, hn_ref, *, act):
    @pl.when(pl.program_id(1) == 0)
    def _():
        hn_ref[...] = _rms(x_ref[...], g_ref[...]).astype(BF16)

    y = jnp.dot(hn_ref[...], w_ref[...], preferred_element_type=F32)
    o_ref[...] = act(y).astype(o_ref.dtype)


def _up_proj(h, g, w, act, tm, tn):
    t, d = h.shape
    n = w.shape[1]
    return pl.pallas_call(
        functools.partial(_up_kernel, act=act),
        grid=(t // tm, n // tn),
        in_specs=[pl.BlockSpec((tm, d), lambda i, j: (i, 0)),
                  pl.BlockSpec((1, d), lambda i, j: (0, 0)),
                  pl.BlockSpec((d, tn), lambda i, j: (0, j))],
        out_specs=pl.BlockSpec((tm, tn), lambda i, j: (i, j)),
        out_shape=jax.ShapeDtypeStruct((t, n), BF16),
        scratch_shapes=[pltpu.VMEM((tm, d), BF16)],
        compiler_params=_params("parallel", "arbitrary"),
        name="up_proj",
    )(h, g, w)


def _up_stats_kernel(x_ref, g_ref, w_ref, o_ref, mean_ref, rstd_ref, hn_ref, s1_ref, s2_ref,
                     *, n_tiles, half):
    j = pl.program_id(1)

    @pl.when(j == 0)
    def _():
        hn_ref[...] = _rms(x_ref[...], g_ref[...]).astype(BF16)
        s1_ref[...] = jnp.zeros_like(s1_ref)
        s2_ref[...] = jnp.zeros_like(s2_ref)

    z = jax.nn.gelu(jnp.dot(hn_ref[...], w_ref[...], preferred_element_type=F32))
    o_ref[...] = z.astype(o_ref.dtype)

    @pl.when(j >= n_tiles // 2)
    def _():
        s1_ref[...] += jnp.sum(z, axis=-1, keepdims=True)
        s2_ref[...] += jnp.sum(z * z, axis=-1, keepdims=True)

    @pl.when(j == n_tiles - 1)
    def _():
        mean = s1_ref[...] * (1.0 / half)
        var = s2_ref[...] * (1.0 / half) - mean * mean
        mean_ref[...] = mean
        rstd_ref[...] = lax.rsqrt(var + EPS)


def _sgu_in(h, g, w, tm, tn):
    t, d = h.shape
    n = w.shape[1]
    n_tiles = n // tn
    return pl.pallas_call(
        functools.partial(_up_stats_kernel, n_tiles=n_tiles, half=n // 2),
        grid=(t // tm, n_tiles),
        in_specs=[pl.BlockSpec((tm, d), lambda i, j: (i, 0)),
                  pl.BlockSpec((1, d), lambda i, j: (0, 0)),
                  pl.BlockSpec((d, tn), lambda i, j: (0, j))],
        out_specs=[pl.BlockSpec((tm, tn), lambda i, j: (i, j)),
                   pl.BlockSpec((tm, 1), lambda i, j: (i, 0)),
                   pl.BlockSpec((tm, 1), lambda i, j: (i, 0))],
        out_shape=[jax.ShapeDtypeStruct((t, n), BF16),
                   jax.ShapeDtypeStruct((t, 1), F32),
                   jax.ShapeDtypeStruct((t, 1), F32)],
        scratch_shapes=[pltpu.VMEM((tm, d), BF16), pltpu.VMEM((tm, 1), F32),
                        pltpu.VMEM((tm, 1), F32)],
        compiler_params=_params("parallel", "arbitrary"),
        name="sgu_in",
    )(h, g, w)


def _down_kernel(a_ref, w_ref, h_ref, g_ref, o_ref, acc_ref, *, n_k):
    k = pl.program_id(1)

    @pl.when(k == 0)
    def _():
        acc_ref[...] = jnp.zeros_like(acc_ref)

    acc_ref[...] += jnp.dot(a_ref[...], w_ref[...], preferred_element_type=F32)

    @pl.when(k == n_k - 1)
    def _():
        o_ref[...] = h_ref[...] + _rms(acc_ref[...], g_ref[...])


def _down_proj(a, w, h, g, tm, tk):
    t, kk = a.shape
    d = w.shape[1]
    n_k = kk // tk
    return pl.pallas_call(
        functools.partial(_down_kernel, n_k=n_k),
        grid=(t // tm, n_k),
        in_specs=[pl.BlockSpec((tm, tk), lambda i, k: (i, k)),
                  pl.BlockSpec((tk, d), lambda i, k: (k, 0)),
                  pl.BlockSpec((tm, d), lambda i, k: (i, 0)),
                  pl.BlockSpec((1, d), lambda i, k: (0, 0))],
        out_specs=pl.BlockSpec((tm, d), lambda i, k: (i, 0)),
        out_shape=jax.ShapeDtypeStruct((t, d), F32),
        scratch_shapes=[pltpu.VMEM((tm, d), F32)],
        compiler_params=_params("parallel", "arbitrary"),
        name="down_proj",
    )(a, w, h, g)


def _sgu_gate_kernel(u_ref, v_ref, mean_ref, rstd_ref, lg_ref, lb_ref, w_ref, b_ref, o_ref):
    tm = u_ref.shape[0]
    row = lax.broadcasted_iota(jnp.int32, (SGU_BLOCK, SGU_BLOCK), 0)
    col = lax.broadcasted_iota(jnp.int32, (SGU_BLOCK, SGU_BLOCK), 1)
    w = jnp.where(col // SEQ_CHUNK <= row // SEQ_CHUNK, w_ref[...], 0.0).astype(BF16)
    lg = lg_ref[...]
    lb = lb_ref[...]
    bias = b_ref[...]
    for n in range(tm // SGU_BLOCK):
        rows = pl.ds(n * SGU_BLOCK, SGU_BLOCK)
        v = v_ref[rows, :].astype(F32)
        vn = ((v - mean_ref[rows, :]) * rstd_ref[rows, :] * lg + lb).astype(BF16)
        s = jnp.dot(w, vn, preferred_element_type=F32) + bias
        o_ref[rows, :] = (u_ref[rows, :].astype(F32) * s).astype(o_ref.dtype)


def _sgu_gate(z, mean, rstd, ln_g, ln_b, w_s, b_s, tm):
    t, n = z.shape
    half = n // 2
    heads = w_s.shape[0]
    hd = half // heads
    return pl.pallas_call(
        _sgu_gate_kernel,
        grid=(t // tm, heads),
        in_specs=[pl.BlockSpec((tm, hd), lambda i, hh: (i, hh)),
                  pl.BlockSpec((tm, hd), lambda i, hh: (i, heads + hh)),
                  pl.BlockSpec((tm, 1), lambda i, hh: (i, 0)),
                  pl.BlockSpec((tm, 1), lambda i, hh: (i, 0)),
                  pl.BlockSpec((1, hd), lambda i, hh: (0, hh)),
                  pl.BlockSpec((1, hd), lambda i, hh: (0, hh)),
                  pl.BlockSpec((None, SGU_BLOCK, SGU_BLOCK), lambda i, hh: (hh, 0, 0)),
                  pl.BlockSpec((None, SGU_BLOCK, 1), lambda i, hh: (hh, 0, 0))],
        out_specs=pl.BlockSpec((tm, hd), lambda i, hh: (i, hh)),
        out_shape=jax.ShapeDtypeStruct((t, half), BF16),
        compiler_params=_params("parallel", "arbitrary"),
        name="sgu_gate",
    )(z, z, mean, rstd, ln_g, ln_b, w_s, b_s)


def _zoh(a_re, a_im, log_dt):
    lam_re = jnp.minimum(a_re, EIG_CLIP)
    lam_im = a_im
    dt = jnp.exp(log_dt)
    mag = jnp.exp(lam_re * dt)
    ab_re = mag * jnp.cos(lam_im * dt)
    ab_im = mag * jnp.sin(lam_im * dt)
    denom = lam_re * lam_re + lam_im * lam_im
    coef_re = ((ab_re - 1.0) * lam_re + ab_im * lam_im) / denom
    coef_im = (ab_im * lam_re - (ab_re - 1.0) * lam_im) / denom
    return ab_re, ab_im, coef_re, coef_im


def _cmul(x_re, x_im, y_re, y_im):
    return x_re * y_re - x_im * y_im, x_re * y_im + x_im * y_re


def _s5_prep_kernel(are3_ref, aim3_ref, ldt3_ref, are2_ref, aim2_ref, ldt2_ref,
                    bre_ref, bim_ref, cre4_ref, cim4_ref, cre3_ref, cim3_ref,
                    kd_ref, zre_ref, zim_ref, cmr_ref, cmi_ref, apr_ref, api_ref):
    ab_re, ab_im, coef_re, coef_im = _zoh(are3_ref[...], aim3_ref[...], ldt3_ref[...])
    bb_re, bb_im = _cmul(coef_re, coef_im, bre_ref[...], bim_ref[...])
    n_out = cre4_ref.shape[0]
    pw_re = jnp.ones_like(ab_re)
    pw_im = jnp.zeros_like(ab_im)
    for d in range(SSM_CHUNK):
        w_re, w_im = _cmul(pw_re, pw_im, bb_re, bb_im)
        zre_ref[SSM_CHUNK - 1 - d] = w_re
        zim_ref[SSM_CHUNK - 1 - d] = w_im

        def lag_row(co, carry, w_re=w_re, w_im=w_im, d=d):
            kd_ref[d, co] = jnp.sum(cre4_ref[co] * w_re - cim4_ref[co] * w_im, axis=0)
            return carry

        lax.fori_loop(0, n_out, lag_row, 0)
        pw_re, pw_im = _cmul(pw_re, pw_im, ab_re, ab_im)

    ab_re, ab_im, _, _ = _zoh(are2_ref[...], aim2_ref[...], ldt2_ref[...])
    pw_re, pw_im = ab_re, ab_im
    for d in range(SSM_CHUNK):
        cmr_ref[d] = cre3_ref[...] * pw_re - cim3_ref[...] * pw_im
        cmi_ref[d] = -(cre3_ref[...] * pw_im + cim3_ref[...] * pw_re)
        if d + 1 < SSM_CHUNK:
            pw_re, pw_im = _cmul(pw_re, pw_im, ab_re, ab_im)
    apr_ref[...] = pw_re
    api_ref[...] = pw_im


def _s5_prep(a_re, a_im, log_dt, b_re, b_im, c_re, c_im):
    g, p = a_re.shape
    gc = b_re.shape[-1]
    b_re = jnp.transpose(b_re, (1, 2, 0))
    b_im = jnp.transpose(b_im, (1, 2, 0))
    c_re = jnp.transpose(c_re, (1, 2, 0))
    c_im = jnp.transpose(c_im, (1, 2, 0))
    tc = SSM_CHUNK
    out_shape = [jax.ShapeDtypeStruct((tc, gc, gc, g), F32),
                 jax.ShapeDtypeStruct((tc, p, gc, g), F32),
                 jax.ShapeDtypeStruct((tc, p, gc, g), F32),
                 jax.ShapeDtypeStruct((tc, gc, p, g), F32),
                 jax.ShapeDtypeStruct((tc, gc, p, g), F32),
                 jax.ShapeDtypeStruct((p, g), F32),
                 jax.ShapeDtypeStruct((p, g), F32)]
    return pl.pallas_call(
        _s5_prep_kernel,
        out_shape=out_shape,
        compiler_params=pltpu.CompilerParams(vmem_limit_bytes=VMEM_LIMIT),
        name="s5_prep",
    )(a_re.T.reshape(p, 1, g), a_im.T.reshape(p, 1, g), log_dt.reshape(1, 1, g),
      a_re.T, a_im.T, log_dt.reshape(1, g),
      b_re, b_im, c_re.reshape(gc, p, 1, g), c_im.reshape(gc, p, 1, g), c_re, c_im)


def _s5_tables(kd, zre, zim, cmr, cmi, apr, api):
    tc, gc, _, g = kd.shape
    p = zre.shape[1]
    nj = g // GROUPS_PER_TILE
    gl = GROUPS_PER_TILE
    eye = jnp.eye(gl, dtype=F32)
    kd6 = kd.reshape(tc, gc, gc, nj, gl)
    dblk = jnp.einsum('doijg,gh->jdgiho', kd6, eye).reshape(nj, tc, gl * gc, gl * gc)
    lag = jnp.arange(tc)[None, :] - jnp.arange(tc)[:, None]
    m_intra = jnp.where((lag >= 0)[None, :, :, None, None],
                        dblk[:, jnp.maximum(lag, 0)], 0.0)
    m_intra = jnp.transpose(m_intra, (0, 1, 3, 2, 4)).reshape(nj, tc * LANES, tc * LANES)
    z6 = jnp.stack([zre, zim]).reshape(2, tc, p, gc, nj, gl)
    m_state = jnp.einsum('rkpijg,gh->jkgirhp', z6, eye).reshape(nj, tc * LANES, 2 * gl * p)
    c6 = jnp.stack([cmr, cmi]).reshape(2, tc, gc, p, nj, gl)
    m_carry = jnp.einsum('rkopjg,gh->jrgpkho', c6, eye).reshape(nj, 2 * gl * p, tc * LANES)
    a_re = jnp.transpose(apr.reshape(p, nj, gl), (1, 2, 0)).reshape(nj, 1, gl * p)
    a_im = jnp.transpose(api.reshape(p, nj, gl), (1, 2, 0)).reshape(nj, 1, gl * p)
    return m_intra.astype(BF16), m_state.astype(BF16), m_carry.astype(BF16), a_re, a_im


def _s5_in_kernel(x_ref, g_ref, w_ref, u_ref):
    hn = _rms(x_ref[...], g_ref[...]).astype(BF16)
    u_ref[...] = jnp.dot(hn, w_ref[...], preferred_element_type=F32)


def _s5_in(h3, g, w, n_chunks):
    b, r, _ = h3.shape
    d = w.shape[0]
    return pl.pallas_call(
        _s5_in_kernel,
        grid=(b, SSM_CHUNK),
        in_specs=[pl.BlockSpec((None, r, d), lambda bi, k: (bi, 0, k)),
                  pl.BlockSpec((1, d), lambda bi, k: (0, 0)),
                  pl.BlockSpec((d, d), lambda bi, k: (0, 0))],
        out_specs=pl.BlockSpec((None, r, d), lambda bi, k: (k, bi, 0)),
        out_shape=jax.ShapeDtypeStruct((SSM_CHUNK, n_chunks, d), F32),
        compiler_params=_params("parallel", "arbitrary"),
        name="s5_in",
    )(h3, g, w)


def _chunk_rows(u_ref):
    return jnp.concatenate([u_ref[k] for k in range(SSM_CHUNK)], axis=-1).astype(BF16)


def _s5_state_kernel(u_ref, m_ref, z_ref):
    z_ref[...] = jnp.dot(_chunk_rows(u_ref), m_ref[...], preferred_element_type=F32)


def _s5_state_in(u, m_state):
    tc, m, d = u.shape
    nj, kk, ns = m_state.shape
    return pl.pallas_call(
        _s5_state_kernel,
        grid=(nj,),
        in_specs=[pl.BlockSpec((tc, m, LANES), lambda j: (0, 0, j)),
                  pl.BlockSpec((None, kk, ns), lambda j: (j, 0, 0))],
        out_specs=pl.BlockSpec((m, ns), lambda j: (0, j)),
        out_shape=jax.ShapeDtypeStruct((m, nj * ns), F32),
        compiler_params=_params("parallel"),
        name="s5_state_in",
    )(u, m_state)


def _s5_scan_kernel(z_ref, are_ref, aim_ref, s_ref):
    rows, ns = z_ref.shape
    half = ns // 2
    a_re = are_ref[...]
    a_im = aim_ref[...]

    def step(i, carry):
        s_re, s_im = carry
        s_ref[pl.ds(i, 1), :] = jnp.concatenate([s_re, s_im], axis=-1)
        z = z_ref[pl.ds(i, 1), :]
        n_re = a_re * s_re - a_im * s_im + z[:, :half]
        n_im = a_re * s_im + a_im * s_re + z[:, half:]
        return n_re, n_im

    zero = jnp.zeros((1, half), F32)
    lax.fori_loop(0, rows, step, (zero, zero))


def _s5_scan(z, a_re, a_im, rows_per_seq):
    m, _ = z.shape
    nj, _, half = a_re.shape
    ns = 2 * half
    return pl.pallas_call(
        _s5_scan_kernel,
        grid=(nj, m // rows_per_seq),
        in_specs=[pl.BlockSpec((rows_per_seq, ns), lambda j, r: (r, j)),
                  pl.BlockSpec((None, 1, half), lambda j, r: (j, 0, 0)),
                  pl.BlockSpec((None, 1, half), lambda j, r: (j, 0, 0))],
        out_specs=pl.BlockSpec((rows_per_seq, ns), lambda j, r: (r, j)),
        out_shape=jax.ShapeDtypeStruct(z.shape, F32),
        compiler_params=_params("parallel", "parallel"),
        name="s5_scan",
    )(z, a_re, a_im)


def _s5_mix_kernel(u_ref, s_ref, mi_ref, mc_ref, y_ref):
    y = jnp.dot(_chunk_rows(u_ref), mi_ref[...], preferred_element_type=F32)
    y += jnp.dot(s_ref[...].astype(BF16), mc_ref[...], preferred_element_type=F32)
    y_ref[...] = y


def _s5_mix(u, s, m_intra, m_carry):
    tc, m, d = u.shape
    nj, kk, _ = m_intra.shape
    ns = m_carry.shape[1]
    return pl.pallas_call(
        _s5_mix_kernel,
        grid=(nj,),
        in_specs=[pl.BlockSpec((tc, m, LANES), lambda j: (0, 0, j)),
                  pl.BlockSpec((m, ns), lambda j: (0, j)),
                  pl.BlockSpec((None, kk, kk), lambda j: (j, 0, 0)),
                  pl.BlockSpec((None, ns, kk), lambda j: (j, 0, 0))],
        out_specs=pl.BlockSpec((None, m, kk), lambda j: (j, 0, 0)),
        out_shape=jax.ShapeDtypeStruct((nj, m, kk), F32),
        compiler_params=_params("parallel"),
        name="s5_mix",
    )(u, s, m_intra, m_carry)


def _s5_out_kernel(y_ref, u_ref, h_ref, dsk_ref, wg_ref, wo_ref, g_ref, o_ref):
    nj = y_ref.shape[0]
    y = jnp.concatenate([y_ref[j] for j in range(nj)], axis=-1) + dsk_ref[...] * u_ref[...]
    z = jax.nn.gelu(y)
    gate = jax.nn.sigmoid(jnp.dot(z.astype(BF16), wg_ref[...], preferred_element_type=F32))
    m = jnp.dot((z * gate).astype(BF16), wo_ref[...], preferred_element_type=F32)
    o_ref[...] = h_ref[...] + _rms(m, g_ref[...])


def _s5_out(y, u, h3, d_skip, w_glu, w_out, g, tr):
    nj, m, _ = y.shape
    b, r, _ = h3.shape
    d = w_glu.shape[0]
    nr = r // tr
    const = dict(pipeline_mode=pl.Buffered(1))
    return pl.pallas_call(
        _s5_out_kernel,
        grid=(SSM_CHUNK, b, nr),
        in_specs=[pl.BlockSpec((nj, tr, LANES), lambda k, bi, ri: (0, bi * nr + ri, k)),
                  pl.BlockSpec((None, tr, d), lambda k, bi, ri: (k, bi * nr + ri, 0)),
                  pl.BlockSpec((None, tr, d), lambda k, bi, ri: (bi, ri, k)),
                  pl.BlockSpec((1, d), lambda k, bi, ri: (0, 0)),
                  pl.BlockSpec((d, d), lambda k, bi, ri: (0, 0), **const),
                  pl.BlockSpec((d, d), lambda k, bi, ri: (0, 0), **const),
                  pl.BlockSpec((1, d), lambda k, bi, ri: (0, 0))],
        out_specs=pl.BlockSpec((None, tr, d), lambda k, bi, ri: (bi, ri, k)),
        out_shape=jax.ShapeDtypeStruct(h3.shape, F32),
        compiler_params=_params("parallel", "parallel", "arbitrary"),
        name="s5_out",
    )(y, u, h3, d_skip, w_glu, w_out, g)


def _s5_layer(h, g_pre, g_post, w_in, a_re, a_im, log_dt, b_re, b_im, c_re, c_im,
              d_skip, w_glu, w_out):
    b, l, d = h.shape
    r = l // SSM_CHUNK
    tables = _s5_tables(*_s5_prep(a_re, a_im, log_dt, b_re, b_im, c_re, c_im))
    m_intra, m_state, m_carry, pa_re, pa_im = tables
    h3 = h.reshape(b, r, SSM_CHUNK * d)
    u = _s5_in(h3, g_pre, w_in, b * r)
    z = _s5_state_in(u, m_state)
    s = _s5_scan(z, pa_re, pa_im, r)
    y = _s5_mix(u, s, m_intra, m_carry)
    out = _s5_out(y, u, h3, d_skip, w_glu, w_out, g_post, min(r, 256))
    return out.reshape(b, l, d)


def _tile(n, pref):
    t = min(n, pref)
    assert n % t == 0, (n, t)
    return t


def kernel(x, norm_g, s5_w_in, s5_a_re, s5_a_im, s5_log_dt, s5_b_re, s5_b_im, s5_c_re, s5_c_im,
           s5_d, s5_w_glu, s5_w_out, sgu_w_in, sgu_ln_g, sgu_ln_b, sgu_w_s, sgu_b_s, sgu_w_out,
           ffn_w_up, ffn_w_down):
    bsz, seq, d = x.shape
    t = bsz * seq
    depth = norm_g.shape[0]
    assert seq % SGU_BLOCK == 0 and seq % SSM_CHUNK == 0 and d % LANES == 0
    h = x
    for i in range(depth):
        g = norm_g[i].reshape(4, 1, d)
        j = i // 2
        if i % 2 == 0:
            h = _s5_layer(h, g[0], g[1], s5_w_in[j].astype(BF16), s5_a_re[j], s5_a_im[j],
                          s5_log_dt[j], s5_b_re[j], s5_b_im[j], s5_c_re[j], s5_c_im[j],
                          s5_d[j].reshape(1, d), s5_w_glu[j].astype(BF16),
                          s5_w_out[j].astype(BF16))
            h2 = h.reshape(t, d)
        else:
            h2 = h.reshape(t, d)
            half = sgu_ln_g.shape[-1]
            heads = sgu_w_s.shape[1]
            z, mean, rstd = _sgu_in(h2, g[0], sgu_w_in[j].astype(BF16),
                                    _tile(t, 1024), _tile(half, 1024))
            gated = _sgu_gate(z, mean, rstd, sgu_ln_g[j].reshape(1, half),
                              sgu_ln_b[j].reshape(1, half), sgu_w_s[j],
                              sgu_b_s[j].reshape(heads, SGU_BLOCK, 1), _tile(t, 512))
            h2 = _down_proj(gated, sgu_w_out[j].astype(BF16), h2, g[1],
                            _tile(t, 512), _tile(half, 1024))
        a = _up_proj(h2, g[2], ffn_w_up[i].astype(BF16), _relu2,
                     _tile(t, 1024), _tile(ffn_w_up.shape[-1], 1024))
        h2 = _down_proj(a, ffn_w_down[i].astype(BF16), h2, g[3],
                        _tile(t, 512), _tile(ffn_w_down.shape[1], 1024))
        h = h2.reshape(bsz, seq, d)
    return h
```

```python
import functools

import jax
import jax.numpy as jnp
from jax import lax
from jax.experimental import pallas as pl
from jax.experimental.pallas import tpu as pltpu

F32 = jnp.float32
BF16 = jnp.bfloat16

EPS = 1e-6
EIG_CLIP = -1e-4
SEQ_CHUNK = 64
SGU_BLOCK = 128
S5_GROUP = 16
SSM_CHUNK = 8
LANES = 128
GROUPS_PER_TILE = LANES // S5_GROUP
VMEM_LIMIT = 56 * 1024 * 1024

_dot = functools.partial(jnp.dot, preferred_element_type=F32)
_RESIDENT = dict(pipeline_mode=pl.Buffered(1))


def _params(*sem):
    return pltpu.CompilerParams(dimension_semantics=sem, vmem_limit_bytes=VMEM_LIMIT)


def _rms(x, g):
    ms = jnp.mean(x * x, axis=-1, keepdims=True)
    return x * lax.rsqrt(ms + EPS) * g


def _relu2(y):
    a = jnp.maximum(y, 0.0)
    return a * a


def _tile(n, pref):
    t = min(n, pref)
    assert n % t == 0, (n, t)
    return t


def _up_kernel(x_ref, w_ref, cw_ref, o_ref, cwo_ref, wb_ref, *, act):
    @pl.when(pl.program_id(1) == 0)
    def _():
        wb_ref[...] = w_ref[...].astype(BF16)

    o_ref[...] = act(_dot(x_ref[...], wb_ref[...])).astype(o_ref.dtype)
    cwo_ref[...] = cw_ref[...].astype(BF16)


def _up_proj(hn, w, cast_w, act, tm, tn):
    t, d = hn.shape
    n = w.shape[1]
    nm = t // tm
    steps = (n // tn) * nm
    cr = cast_w.shape[0] // steps
    assert cr * steps == cast_w.shape[0] and cr % 16 == 0
    cc = cast_w.shape[1]
    return pl.pallas_call(
        functools.partial(_up_kernel, act=act),
        grid=(n // tn, nm),
        in_specs=[pl.BlockSpec((tm, d), lambda j, i: (i, 0)),
                  pl.BlockSpec((d, tn), lambda j, i: (0, j)),
                  pl.BlockSpec((cr, cc), lambda j, i: (j * nm + i, 0))],
        out_specs=[pl.BlockSpec((tm, tn), lambda j, i: (i, j)),
                   pl.BlockSpec((cr, cc), lambda j, i: (j * nm + i, 0))],
        out_shape=[jax.ShapeDtypeStruct((t, n), BF16),
                   jax.ShapeDtypeStruct(cast_w.shape, BF16)],
        scratch_shapes=[pltpu.VMEM((d, tn), BF16)],
        compiler_params=_params("arbitrary", "arbitrary"),
        name="up_proj",
    )(hn, w, cast_w)


def _sgu_in_kernel(x_ref, w_ref, cw_ref, o_ref, s1_ref, s2_ref, cwo_ref, wb_ref, *, n_tiles):
    j = pl.program_id(0)
    i = pl.program_id(1)
    tm = x_ref.shape[0]

    @pl.when(i == 0)
    def _():
        wb_ref[...] = w_ref[...].astype(BF16)

    z = jax.nn.gelu(_dot(x_ref[...], wb_ref[...]))
    o_ref[...] = z.astype(o_ref.dtype)
    cwo_ref[...] = cw_ref[...].astype(BF16)
    rows = pl.ds(pl.multiple_of(i * tm, tm), tm)
    first_v = n_tiles // 2

    @pl.when(j == first_v)
    def _():
        s1_ref[rows, :] = jnp.sum(z, axis=-1, keepdims=True)
        s2_ref[rows, :] = jnp.sum(z * z, axis=-1, keepdims=True)

    @pl.when(j > first_v)
    def _():
        s1_ref[rows, :] += jnp.sum(z, axis=-1, keepdims=True)
        s2_ref[rows, :] += jnp.sum(z * z, axis=-1, keepdims=True)


def _sgu_in(hn, w, cast_w, tm, tn):
    t, d = hn.shape
    n = w.shape[1]
    n_tiles = n // tn
    nm = t // tm
    steps = n_tiles * nm
    cr = cast_w.shape[0] // steps
    assert cr * steps == cast_w.shape[0] and cr % 16 == 0 and n_tiles % 2 == 0
    cc = cast_w.shape[1]
    return pl.pallas_call(
        functools.partial(_sgu_in_kernel, n_tiles=n_tiles),
        grid=(n_tiles, nm),
        in_specs=[pl.BlockSpec((tm, d), lambda j, i: (i, 0)),
                  pl.BlockSpec((d, tn), lambda j, i: (0, j)),
                  pl.BlockSpec((cr, cc), lambda j, i: (j * nm + i, 0))],
        out_specs=[pl.BlockSpec((tm, tn), lambda j, i: (i, j)),
                   pl.BlockSpec((t, 1), lambda j, i: (0, 0)),
                   pl.BlockSpec((t, 1), lambda j, i: (0, 0)),
                   pl.BlockSpec((cr, cc), lambda j, i: (j * nm + i, 0))],
        out_shape=[jax.ShapeDtypeStruct((t, n), BF16),
                   jax.ShapeDtypeStruct((t, 1), F32),
                   jax.ShapeDtypeStruct((t, 1), F32),
                   jax.ShapeDtypeStruct(cast_w.shape, BF16)],
        scratch_shapes=[pltpu.VMEM((d, tn), BF16)],
        compiler_params=_params("arbitrary", "arbitrary"),
        name="sgu_in",
    )(hn, w, cast_w)


def _finish(acc, h_ref, g_ref, gn_ref, o_ref, hn_ref):
    h = h_ref[...] + _rms(acc, g_ref[...])
    o_ref[...] = h
    if hn_ref is not None:
        hn_ref[...] = _rms(h, gn_ref[...]).astype(BF16)


def _down_kernel(a_ref, w_ref, h_ref, g_ref, gn_ref, o_ref, *rest, n_k, emit_hn):
    hn_ref, acc_ref = rest if emit_hn else (None, rest[0])
    k = pl.program_id(1)

    @pl.when(k == 0)
    def _():
        acc_ref[...] = jnp.zeros_like(acc_ref)

    acc_ref[...] += _dot(a_ref[...], w_ref[...])

    @pl.when(k == n_k - 1)
    def _():
        _finish(acc_ref[...], h_ref, g_ref, gn_ref, o_ref, hn_ref)


def _down_proj(a, w, h, g, g_next, emit_hn, tm, tk):
    t, kk = a.shape
    d = w.shape[1]
    n_k = kk // tk
    out_specs = [pl.BlockSpec((tm, d), lambda i, k: (i, 0))]
    out_shape = [jax.ShapeDtypeStruct((t, d), F32)]
    if emit_hn:
        out_specs.append(pl.BlockSpec((tm, d), lambda i, k: (i, 0)))
        out_shape.append(jax.ShapeDtypeStruct((t, d), BF16))
    res = pl.pallas_call(
        functools.partial(_down_kernel, n_k=n_k, emit_hn=emit_hn),
        grid=(t // tm, n_k),
        in_specs=[pl.BlockSpec((tm, tk), lambda i, k: (i, k)),
                  pl.BlockSpec((tk, d), lambda i, k: (k, 0)),
                  pl.BlockSpec((tm, d), lambda i, k: (i, 0), **_RESIDENT),
                  pl.BlockSpec((1, d), lambda i, k: (0, 0)),
                  pl.BlockSpec((1, d), lambda i, k: (0, 0))],
        out_specs=out_specs,
        out_shape=out_shape,
        scratch_shapes=[pltpu.VMEM((tm, d), F32)],
        compiler_params=_params("parallel", "arbitrary"),
        name="down_proj",
    )(a, w, h, g, g_next)
    return res if emit_hn else (res[0], None)


def _sgu_down_kernel(u_ref, v_ref, s1_ref, s2_ref, lg_ref, lb_ref, ws_ref, bs_ref, w_ref,
                     h_ref, g_ref, gn_ref, o_ref, hn_ref, acc_ref, gate_ref, *, n_k, half):
    k = pl.program_id(1)
    tm = u_ref.shape[0]
    heads_per_step = ws_ref.shape[0]
    hd = u_ref.shape[1] // heads_per_step

    @pl.when(k == 0)
    def _():
        acc_ref[...] = jnp.zeros_like(acc_ref)

    row = lax.broadcasted_iota(jnp.int32, (SGU_BLOCK, SGU_BLOCK), 0)
    col = lax.broadcasted_iota(jnp.int32, (SGU_BLOCK, SGU_BLOCK), 1)
    causal = col // SEQ_CHUNK <= row // SEQ_CHUNK
    mean = s1_ref[...] * (1.0 / half)
    rstd = lax.rsqrt(s2_ref[...] * (1.0 / half) - mean * mean + EPS)
    for hh in range(heads_per_step):
        cols = pl.ds(hh * hd, hd)
        w = jnp.where(causal, ws_ref[hh], 0.0).astype(BF16)
        lg = lg_ref[:, cols]
        lb = lb_ref[:, cols]
        bias = bs_ref[hh]
        for n in range(tm // SGU_BLOCK):
            rows = pl.ds(n * SGU_BLOCK, SGU_BLOCK)
            v = v_ref[rows, cols].astype(F32)
            vn = ((v - mean[n * SGU_BLOCK:(n + 1) * SGU_BLOCK]) *
                  rstd[n * SGU_BLOCK:(n + 1) * SGU_BLOCK] * lg + lb).astype(BF16)
            s = _dot(w, vn) + bias
            gate_ref[rows, cols] = (u_ref[rows, cols].astype(F32) * s).astype(BF16)

    acc_ref[...] += _dot(gate_ref[...], w_ref[...])

    @pl.when(k == n_k - 1)
    def _():
        _finish(acc_ref[...], h_ref, g_ref, gn_ref, o_ref, hn_ref)


def _sgu_down(z, s1, s2, ln_g, ln_b, w_s, b_s, w_out, h, g, g_next, tm, heads_per_step):
    t, n = z.shape
    half = n // 2
    heads = w_s.shape[0]
    hd = half // heads
    tk = heads_per_step * hd
    n_k = heads // heads_per_step
    d = w_out.shape[1]
    return pl.pallas_call(
        functools.partial(_sgu_down_kernel, n_k=n_k, half=half),
        grid=(t // tm, n_k),
        in_specs=[pl.BlockSpec((tm, tk), lambda i, k: (i, k)),
                  pl.BlockSpec((tm, tk), lambda i, k: (i, n_k + k)),
                  pl.BlockSpec((tm, 1), lambda i, k: (i, 0)),
                  pl.BlockSpec((tm, 1), lambda i, k: (i, 0)),
                  pl.BlockSpec((1, tk), lambda i, k: (0, k)),
                  pl.BlockSpec((1, tk), lambda i, k: (0, k)),
                  pl.BlockSpec((heads_per_step, SGU_BLOCK, SGU_BLOCK), lambda i, k: (k, 0, 0)),
                  pl.BlockSpec((heads_per_step, SGU_BLOCK, 1), lambda i, k: (k, 0, 0)),
                  pl.BlockSpec((tk, d), lambda i, k: (k, 0)),
                  pl.BlockSpec((tm, d), lambda i, k: (i, 0), **_RESIDENT),
                  pl.BlockSpec((1, d), lambda i, k: (0, 0)),
                  pl.BlockSpec((1, d), lambda i, k: (0, 0))],
        out_specs=[pl.BlockSpec((tm, d), lambda i, k: (i, 0)),
                   pl.BlockSpec((tm, d), lambda i, k: (i, 0))],
        out_shape=[jax.ShapeDtypeStruct((t, d), F32), jax.ShapeDtypeStruct((t, d), BF16)],
        scratch_shapes=[pltpu.VMEM((tm, d), F32), pltpu.VMEM((tm, tk), BF16)],
        compiler_params=_params("parallel", "arbitrary"),
        name="sgu_down",
    )(z, z, s1, s2, ln_g, ln_b, w_s, b_s, w_out, h, g, g_next)


def _zoh(a_re, a_im, log_dt):
    lam_re = jnp.minimum(a_re, EIG_CLIP)
    lam_im = a_im
    dt = jnp.exp(log_dt)
    mag = jnp.exp(lam_re * dt)
    ab_re = mag * jnp.cos(lam_im * dt)
    ab_im = mag * jnp.sin(lam_im * dt)
    denom = lam_re * lam_re + lam_im * lam_im
    coef_re = ((ab_re - 1.0) * lam_re + ab_im * lam_im) / denom
    coef_im = (ab_im * lam_re - (ab_re - 1.0) * lam_im) / denom
    return ab_re, ab_im, coef_re, coef_im


def _cmul(x_re, x_im, y_re, y_im):
    return x_re * y_re - x_im * y_im, x_re * y_im + x_im * y_re


def _dot_3pass(a, b):
    a_hi = a.astype(BF16)
    a_lo = (a - a_hi.astype(F32)).astype(BF16)
    b_hi = b.astype(BF16)
    b_lo = (b - b_hi.astype(F32)).astype(BF16)
    return _dot(a_hi, b_hi) + _dot(a_hi, b_lo) + _dot(a_lo, b_hi)


def _s5_tables_kernel(are_ref, aim_ref, ldt_ref, bc1_ref, bc2_ref, cc_ref,
                      areb_ref, aimb_ref, ldtb_ref,
                      mi_ref, ms_ref, mc_ref, apr_ref, api_ref):
    tc, gl, gc, half = SSM_CHUNK, GROUPS_PER_TILE, S5_GROUP, LANES // 2
    lane = lax.broadcasted_iota(jnp.int32, (gc, LANES), 1)
    lower = lane < half
    sgn = jnp.where(lower, -1.0, 1.0)

    ab_re, ab_im, x_re, x_im = _zoh(are_ref[...], aim_ref[...], ldt_ref[...])
    bc1 = bc1_ref[...]
    bc2 = bc2_ref[...]
    ms_ref[...] = jnp.zeros(ms_ref.shape, ms_ref.dtype)
    w_lags = []
    for d in range(tc):
        w = x_re * bc1 + x_im * (sgn * bc2)
        wsw = x_re * bc2 - x_im * (sgn * bc1)
        w_lags.append(w)
        k = tc - 1 - d
        for g in range(gl):
            keep = lower if g % 2 == 0 else jnp.logical_not(lower)
            re_src, im_src = (w[g], wsw[g]) if g % 2 == 0 else (wsw[g], w[g])
            rows = pl.ds(k * LANES + g * gc, gc)
            ms_ref[rows, pl.ds((g // 2) * LANES, LANES)] = jnp.where(keep, re_src, 0.0).astype(BF16)
            ms_ref[rows, pl.ds((gl // 2 + g // 2) * LANES, LANES)] = (
                jnp.where(keep, im_src, 0.0).astype(BF16))
        x_re, x_im = _cmul(x_re, x_im, ab_re, ab_im)

    pw_re, pw_im = ab_re, ab_im
    for _ in range(tc - 1):
        pw_re, pw_im = _cmul(pw_re, pw_im, ab_re, ab_im)
    apr_ref[...] = pw_re
    api_ref[...] = pw_im

    row = lax.broadcasted_iota(jnp.int32, (LANES, LANES), 0)
    lane_sq = lax.broadcasted_iota(jnp.int32, (LANES, LANES), 1)
    c_sign = jnp.where(row < half, 1.0, -1.0)
    k_rows = []
    for g in range(gl):
        w_stack = jnp.concatenate([w_lags[d][g] for d in range(tc)], axis=0)
        k_g = _dot_3pass(w_stack, cc_ref[g] * c_sign)
        k_rows.append(jnp.where(lane_sq // gc == g, k_g, 0.0))
    zero = jnp.zeros((LANES, LANES), BF16)
    lag_blocks = [jnp.concatenate([k_rows[g][d * gc:(d + 1) * gc] for g in range(gl)],
                                  axis=0).astype(BF16) for d in range(tc)]
    for k in range(tc):
        for kp in range(tc):
            mi_ref[pl.ds(k * LANES, LANES), pl.ds(kp * LANES, LANES)] = (
                lag_blocks[kp - k] if kp >= k else zero)

    ab_re, ab_im, _, _ = _zoh(areb_ref[...], aimb_ref[...], ldtb_ref[...])
    c_re = cc_ref[:, :half, :]
    c_im = cc_ref[:, half:, :]
    grp = lax.broadcasted_iota(jnp.int32, (gl, half, LANES), 0)
    lane3 = lax.broadcasted_iota(jnp.int32, (gl, half, LANES), 2)
    diag = lane3 // gc == grp
    q_re, q_im = ab_re, ab_im
    for kp in range(tc):
        cm_re = jnp.where(diag, c_re * q_re - c_im * q_im, 0.0).astype(BF16)
        cm_im = jnp.where(diag, -(c_re * q_im + c_im * q_re), 0.0).astype(BF16)
        for g in range(gl):
            mc_ref[pl.ds(g * half, half), pl.ds(kp * LANES, LANES)] = cm_re[g]
            mc_ref[pl.ds((gl + g) * half, half), pl.ds(kp * LANES, LANES)] = cm_im[g]
        if kp + 1 < tc:
            q_re, q_im = _cmul(q_re, q_im, ab_re, ab_im)


def _s5_tables(a_re, a_im, log_dt, b_re, b_im, c_re, c_im):
    g, p = a_re.shape
    gc = b_re.shape[-1]
    gl = GROUPS_PER_TILE
    assert 2 * p == LANES and gc == S5_GROUP and g % gl == 0
    nj = g // gl
    kk = SSM_CHUNK * LANES
    ns = 2 * gl * p
    are_t = jnp.concatenate([a_re, a_re], axis=-1).reshape(g, 1, LANES)
    aim_t = jnp.concatenate([a_im, a_im], axis=-1).reshape(g, 1, LANES)
    ldt_t = jnp.broadcast_to(log_dt[:, None, None], (g, 1, LANES))
    bt_re = jnp.swapaxes(b_re, 1, 2)
    bt_im = jnp.swapaxes(b_im, 1, 2)
    bc1 = jnp.concatenate([bt_re, bt_im], axis=-1)
    bc2 = jnp.concatenate([bt_im, bt_re], axis=-1)
    ct = jnp.concatenate([jnp.swapaxes(c_re, 1, 2), jnp.swapaxes(c_im, 1, 2)], axis=1)
    cc = jnp.tile(ct, (1, 1, gl))
    lane_blk = lambda r, c: pl.BlockSpec((gl, r, c), lambda j: (j, 0, 0))
    mi, ms, mc, apr, api = pl.pallas_call(
        _s5_tables_kernel,
        grid=(nj,),
        in_specs=[lane_blk(1, LANES), lane_blk(1, LANES), lane_blk(1, LANES),
                  lane_blk(gc, LANES), lane_blk(gc, LANES), lane_blk(2 * p, LANES),
                  lane_blk(p, 1), lane_blk(p, 1), lane_blk(1, 1)],
        out_specs=[pl.BlockSpec((None, kk, kk), lambda j: (j, 0, 0)),
                   pl.BlockSpec((None, kk, ns), lambda j: (j, 0, 0)),
                   pl.BlockSpec((None, ns, kk), lambda j: (j, 0, 0)),
                   lane_blk(1, LANES), lane_blk(1, LANES)],
        out_shape=[jax.ShapeDtypeStruct((nj, kk, kk), BF16),
                   jax.ShapeDtypeStruct((nj, kk, ns), BF16),
                   jax.ShapeDtypeStruct((nj, ns, kk), BF16),
                   jax.ShapeDtypeStruct((g, 1, LANES), F32),
                   jax.ShapeDtypeStruct((g, 1, LANES), F32)],
        compiler_params=_params("parallel"),
        name="s5_tables",
    )(are_t, aim_t, ldt_t, bc1, bc2, cc, a_re[:, :, None], a_im[:, :, None],
      log_dt[:, None, None])
    pa_re = apr[:, 0, :p].reshape(1, g * p)
    pa_im = api[:, 0, :p].reshape(1, g * p)
    return mi, ms, mc, pa_re, pa_im


def _s5_in_kernel(x_ref, g_ref, w_ref, cg_ref, co_ref, u_ref, cgo_ref, coo_ref, wb_ref,
                  *, apply_norm):
    @pl.when((pl.program_id(0) == 0) & (pl.program_id(1) == 0))
    def _():
        wb_ref[...] = w_ref[...].astype(BF16)

    hn = _rms(x_ref[...], g_ref[...]).astype(BF16) if apply_norm else x_ref[...]
    u_ref[...] = _dot(hn, wb_ref[...])
    cgo_ref[...] = cg_ref[...].astype(BF16)
    coo_ref[...] = co_ref[...].astype(BF16)


def _s5_in(x3, g, w, w_glu, w_out, n_chunks, apply_norm):
    b, r, _ = x3.shape
    d = w.shape[0]
    steps = b * SSM_CHUNK
    cr = d // steps
    assert cr * steps == d and cr % 16 == 0
    cast_blk = pl.BlockSpec((cr, d), lambda bi, k: (bi * SSM_CHUNK + k, 0))
    return pl.pallas_call(
        functools.partial(_s5_in_kernel, apply_norm=apply_norm),
        grid=(b, SSM_CHUNK),
        in_specs=[pl.BlockSpec((None, r, d), lambda bi, k: (bi, 0, k)),
                  pl.BlockSpec((1, d), lambda bi, k: (0, 0)),
                  pl.BlockSpec((d, d), lambda bi, k: (0, 0), **_RESIDENT),
                  cast_blk, cast_blk],
        out_specs=[pl.BlockSpec((None, r, d), lambda bi, k: (k, bi, 0)), cast_blk, cast_blk],
        out_shape=[jax.ShapeDtypeStruct((SSM_CHUNK, n_chunks, d), F32),
                   jax.ShapeDtypeStruct((d, d), BF16), jax.ShapeDtypeStruct((d, d), BF16)],
        scratch_shapes=[pltpu.VMEM((d, d), BF16)],
        compiler_params=_params("arbitrary", "arbitrary"),
        name="s5_in",
    )(x3, g, w, w_glu, w_out)


def _chunk_rows(u_ref):
    return jnp.concatenate([u_ref[k] for k in range(SSM_CHUNK)], axis=-1).astype(BF16)


def _s5_state_kernel(u_ref, m_ref, zre_ref, zim_ref):
    z = _dot(_chunk_rows(u_ref), m_ref[...])
    half = z.shape[1] // 2
    zre_ref[...] = z[:, :half]
    zim_ref[...] = z[:, half:]


def _s5_state_in(u, m_state):
    tc, m, d = u.shape
    nj, kk, ns = m_state.shape
    half = ns // 2
    return pl.pallas_call(
        _s5_state_kernel,
        grid=(nj,),
        in_specs=[pl.BlockSpec((tc, m, LANES), lambda j: (0, 0, j)),
                  pl.BlockSpec((None, kk, ns), lambda j: (j, 0, 0))],
        out_specs=[pl.BlockSpec((m, half), lambda j: (0, j)),
                   pl.BlockSpec((m, half), lambda j: (0, j))],
        out_shape=[jax.ShapeDtypeStruct((m, nj * half), F32)] * 2,
        compiler_params=_params("parallel"),
        name="s5_state_in",
    )(u, m_state)


def _s5_scan_kernel(zre_ref, zim_ref, are_ref, aim_ref, sre_ref, sim_ref, st_re, st_im):
    rows = zre_ref.shape[0]

    @pl.when(pl.program_id(1) == 0)
    def _():
        st_re[...] = jnp.zeros_like(st_re)
        st_im[...] = jnp.zeros_like(st_im)

    a_re = are_ref[...]
    a_im = aim_ref[...]

    def step(i, carry):
        s_re, s_im = carry
        row = pl.ds(i, 1)
        sre_ref[row, :] = s_re
        sim_ref[row, :] = s_im
        n_re = a_re * s_re - a_im * s_im + zre_ref[row, :]
        n_im = a_re * s_im + a_im * s_re + zim_ref[row, :]
        return n_re, n_im

    s_re, s_im = lax.fori_loop(0, rows, step, (st_re[...], st_im[...]))
    st_re[...] = s_re
    st_im[...] = s_im


def _s5_scan(z_re, z_im, a_re, a_im, rows_per_seq, tr):
    m, n = z_re.shape
    blk = pl.BlockSpec((tr, n), lambda s, r: (s * (rows_per_seq // tr) + r, 0))
    coef = pl.BlockSpec((1, n), lambda s, r: (0, 0))
    return pl.pallas_call(
        _s5_scan_kernel,
        grid=(m // rows_per_seq, rows_per_seq // tr),
        in_specs=[blk, blk, coef, coef],
        out_specs=[blk, blk],
        out_shape=[jax.ShapeDtypeStruct((m, n), F32)] * 2,
        scratch_shapes=[pltpu.VMEM((1, n), F32), pltpu.VMEM((1, n), F32)],
        compiler_params=_params("arbitrary", "arbitrary"),
        name="s5_scan",
    )(z_re, z_im, a_re, a_im)


def _s5_mix_kernel(u_ref, sre_ref, sim_ref, mi_ref, mc_ref, y_ref):
    y = _dot(_chunk_rows(u_ref), mi_ref[...])
    s = jnp.concatenate([sre_ref[...], sim_ref[...]], axis=-1).astype(BF16)
    y_ref[...] = y + _dot(s, mc_ref[...])


def _s5_mix(u, s_re, s_im, m_intra, m_carry):
    tc, m, d = u.shape
    nj, kk, _ = m_intra.shape
    ns = m_carry.shape[1]
    half = ns // 2
    return pl.pallas_call(
        _s5_mix_kernel,
        grid=(nj,),
        in_specs=[pl.BlockSpec((tc, m, LANES), lambda j: (0, 0, j)),
                  pl.BlockSpec((m, half), lambda j: (0, j)),
                  pl.BlockSpec((m, half), lambda j: (0, j)),
                  pl.BlockSpec((None, kk, kk), lambda j: (j, 0, 0)),
                  pl.BlockSpec((None, ns, kk), lambda j: (j, 0, 0))],
        out_specs=pl.BlockSpec((None, m, kk), lambda j: (j, 0, 0)),
        out_shape=jax.ShapeDtypeStruct((nj, m, kk), F32),
        compiler_params=_params("parallel"),
        name="s5_mix",
    )(u, s_re, s_im, m_intra, m_carry)


def _s5_out_kernel(y_ref, u_ref, h_ref, dsk_ref, wg_ref, wo_ref, g_ref, gn_ref, o_ref, hn_ref):
    nj = y_ref.shape[0]
    y = jnp.concatenate([y_ref[j] for j in range(nj)], axis=-1) + dsk_ref[...] * u_ref[...]
    z = jax.nn.gelu(y)
    gate = jax.nn.sigmoid(_dot(z.astype(BF16), wg_ref[...]))
    m = _dot((z * gate).astype(BF16), wo_ref[...])
    _finish(m, h_ref, g_ref, gn_ref, o_ref, hn_ref)


def _s5_out(y, u, h3, d_skip, w_glu, w_out, g, g_next, tr):
    nj, m, _ = y.shape
    b, r, _ = h3.shape
    d = w_glu.shape[0]
    nr = r // tr
    tok = pl.BlockSpec((None, tr, d), lambda k, bi, ri: (bi, ri, k))
    vec = pl.BlockSpec((1, d), lambda k, bi, ri: (0, 0))
    mat = pl.BlockSpec((d, d), lambda k, bi, ri: (0, 0), **_RESIDENT)
    return pl.pallas_call(
        _s5_out_kernel,
        grid=(SSM_CHUNK, b, nr),
        in_specs=[pl.BlockSpec((nj, tr, LANES), lambda k, bi, ri: (0, bi * nr + ri, k)),
                  pl.BlockSpec((None, tr, d), lambda k, bi, ri: (k, bi * nr + ri, 0)),
                  tok, vec, mat, mat, vec, vec],
        out_specs=[tok, tok],
        out_shape=[jax.ShapeDtypeStruct(h3.shape, F32), jax.ShapeDtypeStruct(h3.shape, BF16)],
        compiler_params=_params("parallel", "parallel", "arbitrary"),
        name="s5_out",
    )(y, u, h3, d_skip, w_glu, w_out, g, g_next)


def _s5_layer(h, hn, g_pre, g_post, g_next, w_in, a_re, a_im, log_dt, b_re, b_im, c_re, c_im,
              d_skip, w_glu, w_out):
    b, l, d = h.shape
    r = l // SSM_CHUNK
    m_intra, m_state, m_carry, pa_re, pa_im = _s5_tables(a_re, a_im, log_dt, b_re, b_im,
                                                         c_re, c_im)
    h3 = h.reshape(b, r, SSM_CHUNK * d)
    x3 = h3 if hn is None else hn.reshape(b, r, SSM_CHUNK * d)
    u, w_glu, w_out = _s5_in(x3, g_pre, w_in, w_glu, w_out, b * r, hn is None)
    z_re, z_im = _s5_state_in(u, m_state)
    s_re, s_im = _s5_scan(z_re, z_im, pa_re, pa_im, r, _tile(r, 64))
    y = _s5_mix(u, s_re, s_im, m_intra, m_carry)
    h, hn = _s5_out(y, u, h3, d_skip, w_glu, w_out, g_post, g_next, _tile(r, 256))
    return h.reshape(b, l, d), hn.reshape(b, l, d)


def kernel(x, norm_g, s5_w_in, s5_a_re, s5_a_im, s5_log_dt, s5_b_re, s5_b_im, s5_c_re, s5_c_im,
           s5_d, s5_w_glu, s5_w_out, sgu_w_in, sgu_ln_g, sgu_ln_b, sgu_w_s, sgu_b_s, sgu_w_out,
           ffn_w_up, ffn_w_down):
    bsz, seq, d = x.shape
    t = bsz * seq
    depth = norm_g.shape[0]
    assert seq % SGU_BLOCK == 0 and seq % SSM_CHUNK == 0 and d % LANES == 0
    gains = norm_g.reshape(depth, 4, 1, d)
    h = x
    hn = None
    for i in range(depth):
        g = gains[i]
        j = i // 2
        if i % 2 == 0:
            h, hn = _s5_layer(h, hn, g[0], g[1], g[2], s5_w_in[j], s5_a_re[j], s5_a_im[j],
                              s5_log_dt[j], s5_b_re[j], s5_b_im[j], s5_c_re[j], s5_c_im[j],
                              s5_d[j].reshape(1, d), s5_w_glu[j], s5_w_out[j])
            h2, hn2 = h.reshape(t, d), hn.reshape(t, d)
        else:
            h2, hn2 = h.reshape(t, d), hn.reshape(t, d)
            half = sgu_ln_g.shape[-1]
            heads = sgu_w_s.shape[1]
            z, s1, s2, w_out = _sgu_in(hn2, sgu_w_in[j], sgu_w_out[j],
                                       _tile(t, 1024), _tile(half, 1024))
            h2, hn2 = _sgu_down(z, s1, s2, sgu_ln_g[j].reshape(1, half),
                                sgu_ln_b[j].reshape(1, half), sgu_w_s[j],
                                sgu_b_s[j].reshape(heads, SGU_BLOCK, 1), w_out, h2, g[1], g[2],
                                _tile(t, 512), 2)
        a, w_down = _up_proj(hn2, ffn_w_up[i], ffn_w_down[i], _relu2,
                             _tile(t, 1024), _tile(ffn_w_up.shape[-1], 1024))
        last = i == depth - 1
        h2, hn2 = _down_proj(a, w_down, h2, g[3], g[3] if last else gains[i + 1][0], not last,
                             _tile(t, 1024), _tile(ffn_w_down.shape[1], 512))
        h = h2.reshape(bsz, seq, d)
        hn = None if last else hn2.reshape(bsz, seq, d)
    return h
```

```python
import functools

import jax
import jax.numpy as jnp
from jax import lax
from jax.experimental import pallas as pl
from jax.experimental.pallas import tpu as pltpu

F32 = jnp.float32
BF16 = jnp.bfloat16

EPS = 1e-6
EIG_CLIP = -1e-4
SEQ_CHUNK = 64
SGU_BLOCK = 128
S5_GROUP = 16
SSM_CHUNK = 8
LANES = 128
GROUPS_PER_TILE = LANES // S5_GROUP
VMEM_LIMIT = 56 * 1024 * 1024

_dot = functools.partial(jnp.dot, preferred_element_type=F32)
_RESIDENT = dict(pipeline_mode=pl.Buffered(1))


def _params(*sem):
    return pltpu.CompilerParams(dimension_semantics=sem, vmem_limit_bytes=VMEM_LIMIT)


def _rms(x, g):
    ms = jnp.mean(x * x, axis=-1, keepdims=True)
    return x * lax.rsqrt(ms + EPS) * g


def _relu2(y):
    a = jnp.maximum(y, 0.0)
    return a * a


def _tile(n, pref):
    t = min(n, pref)
    assert n % t == 0, (n, t)
    return t


def _up_kernel(x_ref, w_ref, cw_ref, o_ref, cwo_ref, wb_ref, *, act):
    @pl.when(pl.program_id(1) == 0)
    def _():
        wb_ref[...] = w_ref[...].astype(BF16)

    o_ref[...] = act(_dot(x_ref[...], wb_ref[...])).astype(o_ref.dtype)
    cwo_ref[...] = cw_ref[...].astype(BF16)


def _cast_rows(cast_w, steps):
    rows = cast_w.shape[1] // steps
    assert rows * steps == cast_w.shape[1] and rows % 16 == 0
    return rows


def _up_proj(hn, w, cast_w, layer, act, tm, tn):
    t, d = hn.shape
    n = w.shape[2]
    nm = t // tm
    cr = _cast_rows(cast_w, (n // tn) * nm)
    cc = cast_w.shape[2]
    return pl.pallas_call(
        functools.partial(_up_kernel, act=act),
        grid=(n // tn, nm),
        in_specs=[pl.BlockSpec((tm, d), lambda j, i: (i, 0)),
                  pl.BlockSpec((None, d, tn), lambda j, i: (layer, 0, j)),
                  pl.BlockSpec((None, cr, cc), lambda j, i: (layer, j * nm + i, 0))],
        out_specs=[pl.BlockSpec((tm, tn), lambda j, i: (i, j)),
                   pl.BlockSpec((cr, cc), lambda j, i: (j * nm + i, 0))],
        out_shape=[jax.ShapeDtypeStruct((t, n), BF16),
                   jax.ShapeDtypeStruct(cast_w.shape[1:], BF16)],
        scratch_shapes=[pltpu.VMEM((d, tn), BF16)],
        compiler_params=_params("arbitrary", "arbitrary"),
        name="up_proj",
    )(hn, w, cast_w)


def _sgu_in_kernel(x_ref, w_ref, cw_ref, o_ref, s1_ref, s2_ref, cwo_ref, wb_ref, *, n_tiles):
    j = pl.program_id(0)
    i = pl.program_id(1)
    tm = x_ref.shape[0]

    @pl.when(i == 0)
    def _():
        wb_ref[...] = w_ref[...].astype(BF16)

    z = jax.nn.gelu(_dot(x_ref[...], wb_ref[...]))
    o_ref[...] = z.astype(o_ref.dtype)
    cwo_ref[...] = cw_ref[...].astype(BF16)
    rows = pl.ds(pl.multiple_of(i * tm, tm), tm)
    first_v = n_tiles // 2

    @pl.when(j == first_v)
    def _():
        s1_ref[rows, :] = jnp.sum(z, axis=-1, keepdims=True)
        s2_ref[rows, :] = jnp.sum(z * z, axis=-1, keepdims=True)

    @pl.when(j > first_v)
    def _():
        s1_ref[rows, :] += jnp.sum(z, axis=-1, keepdims=True)
        s2_ref[rows, :] += jnp.sum(z * z, axis=-1, keepdims=True)


def _sgu_in(hn, w, cast_w, layer, tm, tn):
    t, d = hn.shape
    n = w.shape[2]
    n_tiles = n // tn
    nm = t // tm
    assert n_tiles % 2 == 0
    cr = _cast_rows(cast_w, n_tiles * nm)
    cc = cast_w.shape[2]
    return pl.pallas_call(
        functools.partial(_sgu_in_kernel, n_tiles=n_tiles),
        grid=(n_tiles, nm),
        in_specs=[pl.BlockSpec((tm, d), lambda j, i: (i, 0)),
                  pl.BlockSpec((None, d, tn), lambda j, i: (layer, 0, j)),
                  pl.BlockSpec((None, cr, cc), lambda j, i: (layer, j * nm + i, 0))],
        out_specs=[pl.BlockSpec((tm, tn), lambda j, i: (i, j)),
                   pl.BlockSpec((t, 1), lambda j, i: (0, 0)),
                   pl.BlockSpec((t, 1), lambda j, i: (0, 0)),
                   pl.BlockSpec((cr, cc), lambda j, i: (j * nm + i, 0))],
        out_shape=[jax.ShapeDtypeStruct((t, n), BF16),
                   jax.ShapeDtypeStruct((t, 1), F32),
                   jax.ShapeDtypeStruct((t, 1), F32),
                   jax.ShapeDtypeStruct(cast_w.shape[1:], BF16)],
        scratch_shapes=[pltpu.VMEM((d, tn), BF16)],
        compiler_params=_params("arbitrary", "arbitrary"),
        name="sgu_in",
    )(hn, w, cast_w)


def _finish(f, h_ref, g_ref, gn_ref, o_ref, hn_ref):
    h = h_ref[...].reshape(f.shape) + _rms(f, g_ref[...])
    o_ref[...] = h.reshape(o_ref.shape)
    if hn_ref is not None:
        hn_ref[...] = _rms(h, gn_ref[...]).astype(BF16).reshape(hn_ref.shape)


def _down_kernel(a_ref, w_ref, h_ref, g_ref, gn_ref, o_ref, *rest, n_n, emit_hn):
    hn_ref, f_ref = rest if emit_hn else (None, rest[0])
    n = pl.program_id(1)
    f_ref[n] = _dot(a_ref[...], w_ref[...])

    @pl.when(n == n_n - 1)
    def _():
        f = jnp.concatenate([f_ref[c] for c in range(n_n)], axis=-1)
        _finish(f, h_ref, g_ref, gn_ref, o_ref, hn_ref)


def _down_proj(a, w, h, g, g_next, emit_hn, tm, tn):
    t, kk = a.shape
    d = w.shape[1]
    n_n = d // tn
    out_specs = [pl.BlockSpec((tm, d), lambda i, n: (i, 0))]
    out_shape = [jax.ShapeDtypeStruct((t, d), F32)]
    if emit_hn:
        out_specs.append(pl.BlockSpec((tm, d), lambda i, n: (i, 0)))
        out_shape.append(jax.ShapeDtypeStruct((t, d), BF16))
    res = pl.pallas_call(
        functools.partial(_down_kernel, n_n=n_n, emit_hn=emit_hn),
        grid=(t // tm, n_n),
        in_specs=[pl.BlockSpec((tm, kk), lambda i, n: (i, 0)),
                  pl.BlockSpec((kk, tn), lambda i, n: (0, n)),
                  pl.BlockSpec((tm, d), lambda i, n: (i, 0), **_RESIDENT),
                  pl.BlockSpec((1, d), lambda i, n: (0, 0)),
                  pl.BlockSpec((1, d), lambda i, n: (0, 0))],
        out_specs=out_specs,
        out_shape=out_shape,
        scratch_shapes=[pltpu.VMEM((n_n, tm, tn), F32)],
        compiler_params=_params("parallel", "arbitrary"),
        name="down_proj",
    )(a, w, h, g, g_next)
    return res if emit_hn else (res[0], None)


def _sgu_down_kernel(u_ref, v_ref, s1_ref, s2_ref, lg_ref, lb_ref, ws_ref, bs_ref, w_ref,
                     h_ref, g_ref, gn_ref, o_ref, hn_ref, acc_ref, gate_ref, *, n_k, half):
    k = pl.program_id(1)
    tc, mt, tk = u_ref.shape
    heads_per_step = ws_ref.shape[0]
    hd = tk // heads_per_step
    cpb = SGU_BLOCK // tc
    cps = SEQ_CHUNK // tc

    @pl.when(k == 0)
    def _():
        acc_ref[...] = jnp.zeros_like(acc_ref)

    row = lax.broadcasted_iota(jnp.int32, (SGU_BLOCK, SGU_BLOCK), 0)
    col = lax.broadcasted_iota(jnp.int32, (SGU_BLOCK, SGU_BLOCK), 1)
    causal = (col % cpb) // cps <= (row % cpb) // cps
    mean = s1_ref[...] * (1.0 / half)
    rstd = lax.rsqrt(s2_ref[...] * (1.0 / half) - mean * mean + EPS)

    def block_of(x, n):
        return jnp.concatenate([x[o, n * cpb:(n + 1) * cpb] for o in range(tc)], axis=0)

    for hh in range(heads_per_step):
        cols = pl.ds(hh * hd, hd)
        w = jnp.where(causal, ws_ref[hh], 0.0).astype(BF16)
        lg = lg_ref[:, cols]
        lb = lb_ref[:, cols]
        bias = bs_ref[hh]
        for n in range(mt // cpb):
            rows = pl.ds(n * cpb, cpb)
            v = jnp.concatenate([v_ref[o, rows, cols] for o in range(tc)], axis=0).astype(F32)
            vn = ((v - block_of(mean, n)) * block_of(rstd, n) * lg + lb).astype(BF16)
            s = _dot(w, vn) + bias
            u = jnp.concatenate([u_ref[o, rows, cols] for o in range(tc)], axis=0).astype(F32)
            gated = (u * s).astype(BF16)
            for o in range(tc):
                gate_ref[o, rows, cols] = gated[o * cpb:(o + 1) * cpb]

    acc_ref[...] += _dot(gate_ref[...].reshape(tc * mt, tk), w_ref[...])

    @pl.when(k == n_k - 1)
    def _():
        _finish(acc_ref[...], h_ref, g_ref, gn_ref, o_ref, hn_ref)


def _sgu_down(z, s1, s2, ln_g, ln_b, w_s, b_s, w_out, h, g, g_next, mt, heads_per_step):
    tc, m, n = z.shape
    half = n // 2
    heads = w_s.shape[0]
    hd = half // heads
    tk = heads_per_step * hd
    n_k = heads // heads_per_step
    d = w_out.shape[1]
    tok = lambda c: pl.BlockSpec((tc, mt, c), lambda i, k: (0, i, 0))
    return pl.pallas_call(
        functools.partial(_sgu_down_kernel, n_k=n_k, half=half),
        grid=(m // mt, n_k),
        in_specs=[pl.BlockSpec((tc, mt, tk), lambda i, k: (0, i, k)),
                  pl.BlockSpec((tc, mt, tk), lambda i, k: (0, i, n_k + k)),
                  tok(1), tok(1),
                  pl.BlockSpec((1, tk), lambda i, k: (0, k)),
                  pl.BlockSpec((1, tk), lambda i, k: (0, k)),
                  pl.BlockSpec((heads_per_step, SGU_BLOCK, SGU_BLOCK), lambda i, k: (k, 0, 0)),
                  pl.BlockSpec((heads_per_step, SGU_BLOCK, 1), lambda i, k: (k, 0, 0)),
                  pl.BlockSpec((tk, d), lambda i, k: (k, 0)),
                  pl.BlockSpec((tc, mt, d), lambda i, k: (0, i, 0), **_RESIDENT),
                  pl.BlockSpec((1, d), lambda i, k: (0, 0)),
                  pl.BlockSpec((1, d), lambda i, k: (0, 0))],
        out_specs=[tok(d), tok(d)],
        out_shape=[jax.ShapeDtypeStruct((tc, m, d), F32), jax.ShapeDtypeStruct((tc, m, d), BF16)],
        scratch_shapes=[pltpu.VMEM((tc * mt, d), F32), pltpu.VMEM((tc, mt, tk), BF16)],
        compiler_params=_params("parallel", "arbitrary"),
        name="sgu_down",
    )(z, z, s1, s2, ln_g, ln_b, w_s, b_s, w_out, h, g, g_next)


def _zoh(a_re, a_im, log_dt):
    lam_re = jnp.minimum(a_re, EIG_CLIP)
    lam_im = a_im
    dt = jnp.exp(log_dt)
    mag = jnp.exp(lam_re * dt)
    ab_re = mag * jnp.cos(lam_im * dt)
    ab_im = mag * jnp.sin(lam_im * dt)
    denom = lam_re * lam_re + lam_im * lam_im
    coef_re = ((ab_re - 1.0) * lam_re + ab_im * lam_im) / denom
    coef_im = (ab_im * lam_re - (ab_re - 1.0) * lam_im) / denom
    return ab_re, ab_im, coef_re, coef_im


def _cmul(x_re, x_im, y_re, y_im):
    return x_re * y_re - x_im * y_im, x_re * y_im + x_im * y_re


def _dot_3pass(a, b):
    a_hi = a.astype(BF16)
    a_lo = (a - a_hi.astype(F32)).astype(BF16)
    b_hi = b.astype(BF16)
    b_lo = (b - b_hi.astype(F32)).astype(BF16)
    return _dot(a_hi, b_hi) + _dot(a_hi, b_lo) + _dot(a_lo, b_hi)


def _s5_tables_kernel(are_ref, aim_ref, ldt_ref, bc1_ref, bc2_ref, cc_ref,
                      areb_ref, aimb_ref, ldtb_ref,
                      mi_ref, ms_ref, mc_ref, apr_ref, api_ref):
    tc, gl, gc, half = SSM_CHUNK, GROUPS_PER_TILE, S5_GROUP, LANES // 2
    lane = lax.broadcasted_iota(jnp.int32, (gc, LANES), 1)
    lower = lane < half
    sgn = jnp.where(lower, -1.0, 1.0)

    ab_re, ab_im, x_re, x_im = _zoh(are_ref[...], aim_ref[...], ldt_ref[...])
    bc1 = bc1_ref[...]
    bc2 = bc2_ref[...]
    ms_ref[...] = jnp.zeros(ms_ref.shape, ms_ref.dtype)
    w_lags = []
    for d in range(tc):
        w = x_re * bc1 + x_im * (sgn * bc2)
        wsw = x_re * bc2 - x_im * (sgn * bc1)
        w_lags.append(w)
        k = tc - 1 - d
        for g in range(gl):
            keep = lower if g % 2 == 0 else jnp.logical_not(lower)
            re_src, im_src = (w[g], wsw[g]) if g % 2 == 0 else (wsw[g], w[g])
            rows = pl.ds(k * LANES + g * gc, gc)
            ms_ref[rows, pl.ds((g // 2) * LANES, LANES)] = jnp.where(keep, re_src, 0.0).astype(BF16)
            ms_ref[rows, pl.ds((gl // 2 + g // 2) * LANES, LANES)] = (
                jnp.where(keep, im_src, 0.0).astype(BF16))
        x_re, x_im = _cmul(x_re, x_im, ab_re, ab_im)

    pw_re, pw_im = ab_re, ab_im
    for _ in range(tc - 1):
        pw_re, pw_im = _cmul(pw_re, pw_im, ab_re, ab_im)
    apr_ref[...] = pw_re
    api_ref[...] = pw_im

    row = lax.broadcasted_iota(jnp.int32, (LANES, LANES), 0)
    lane_sq = lax.broadcasted_iota(jnp.int32, (LANES, LANES), 1)
    c_sign = jnp.where(row < half, 1.0, -1.0)
    k_rows = []
    for g in range(gl):
        w_stack = jnp.concatenate([w_lags[d][g] for d in range(tc)], axis=0)
        k_g = _dot_3pass(w_stack, cc_ref[g] * c_sign)
        k_rows.append(jnp.where(lane_sq // gc == g, k_g, 0.0))
    zero = jnp.zeros((LANES, LANES), BF16)
    lag_blocks = [jnp.concatenate([k_rows[g][d * gc:(d + 1) * gc] for g in range(gl)],
                                  axis=0).astype(BF16) for d in range(tc)]
    for k in range(tc):
        for kp in range(tc):
            mi_ref[pl.ds(k * LANES, LANES), pl.ds(kp * LANES, LANES)] = (
                lag_blocks[kp - k] if kp >= k else zero)

    ab_re, ab_im, _, _ = _zoh(areb_ref[...], aimb_ref[...], ldtb_ref[...])
    c_re = cc_ref[:, :half, :]
    c_im = cc_ref[:, half:, :]
    grp = lax.broadcasted_iota(jnp.int32, (gl, half, LANES), 0)
    lane3 = lax.broadcasted_iota(jnp.int32, (gl, half, LANES), 2)
    diag = lane3 // gc == grp
    q_re, q_im = ab_re, ab_im
    for kp in range(tc):
        cm_re = jnp.where(diag, c_re * q_re - c_im * q_im, 0.0).astype(BF16)
        cm_im = jnp.where(diag, -(c_re * q_im + c_im * q_re), 0.0).astype(BF16)
        for g in range(gl):
            mc_ref[pl.ds(g * half, half), pl.ds(kp * LANES, LANES)] = cm_re[g]
            mc_ref[pl.ds((gl + g) * half, half), pl.ds(kp * LANES, LANES)] = cm_im[g]
        if kp + 1 < tc:
            q_re, q_im = _cmul(q_re, q_im, ab_re, ab_im)


def _s5_tables(a_re, a_im, log_dt, b_re, b_im, c_re, c_im):
    g, p = a_re.shape
    gc = b_re.shape[-1]
    gl = GROUPS_PER_TILE
    assert 2 * p == LANES and gc == S5_GROUP and g % gl == 0
    nj = g // gl
    kk = SSM_CHUNK * LANES
    ns = 2 * gl * p
    are_t = jnp.concatenate([a_re, a_re], axis=-1).reshape(g, 1, LANES)
    aim_t = jnp.concatenate([a_im, a_im], axis=-1).reshape(g, 1, LANES)
    ldt_t = jnp.broadcast_to(log_dt[:, None, None], (g, 1, LANES))
    bt_re = jnp.swapaxes(b_re, 1, 2)
    bt_im = jnp.swapaxes(b_im, 1, 2)
    bc1 = jnp.concatenate([bt_re, bt_im], axis=-1)
    bc2 = jnp.concatenate([bt_im, bt_re], axis=-1)
    ct = jnp.concatenate([jnp.swapaxes(c_re, 1, 2), jnp.swapaxes(c_im, 1, 2)], axis=1)
    cc = jnp.tile(ct, (1, 1, gl))
    lane_blk = lambda r, c: pl.BlockSpec((gl, r, c), lambda j: (j, 0, 0))
    mi, ms, mc, apr, api = pl.pallas_call(
        _s5_tables_kernel,
        grid=(nj,),
        in_specs=[lane_blk(1, LANES), lane_blk(1, LANES), lane_blk(1, LANES),
                  lane_blk(gc, LANES), lane_blk(gc, LANES), lane_blk(2 * p, LANES),
                  lane_blk(p, 1), lane_blk(p, 1), lane_blk(1, 1)],
        out_specs=[pl.BlockSpec((None, kk, kk), lambda j: (j, 0, 0)),
                   pl.BlockSpec((None, kk, ns), lambda j: (j, 0, 0)),
                   pl.BlockSpec((None, ns, kk), lambda j: (j, 0, 0)),
                   lane_blk(1, LANES), lane_blk(1, LANES)],
        out_shape=[jax.ShapeDtypeStruct((nj, kk, kk), BF16),
                   jax.ShapeDtypeStruct((nj, kk, ns), BF16),
                   jax.ShapeDtypeStruct((nj, ns, kk), BF16),
                   jax.ShapeDtypeStruct((g, 1, LANES), F32),
                   jax.ShapeDtypeStruct((g, 1, LANES), F32)],
        compiler_params=_params("parallel"),
        name="s5_tables",
    )(are_t, aim_t, ldt_t, bc1, bc2, cc, a_re[:, :, None], a_im[:, :, None],
      log_dt[:, None, None])
    pa_re = apr[:, 0, :p].reshape(1, g * p)
    pa_im = api[:, 0, :p].reshape(1, g * p)
    return mi, ms, mc, pa_re, pa_im


def _s5_in_kernel(x_ref, g_ref, w_ref, cg_ref, co_ref, u_ref, cgo_ref, coo_ref, wb_ref,
                  *, apply_norm):
    @pl.when((pl.program_id(0) == 0) & (pl.program_id(1) == 0))
    def _():
        wb_ref[...] = w_ref[...].astype(BF16)

    hn = _rms(x_ref[...], g_ref[...]).astype(BF16) if apply_norm else x_ref[...]
    u_ref[...] = _dot(hn, wb_ref[...])
    cgo_ref[...] = cg_ref[...].astype(BF16)
    coo_ref[...] = co_ref[...].astype(BF16)


def _s5_in(x, g, w, w_glu, w_out, layer, apply_norm, tr):
    tc, m, d = x.shape
    nr = m // tr
    cr = _cast_rows(w_glu, tc * nr)
    tok = pl.BlockSpec((None, tr, d), lambda k, r: (k, r, 0))
    cast_in = pl.BlockSpec((None, cr, d), lambda k, r: (layer, k * nr + r, 0))
    cast_out = pl.BlockSpec((cr, d), lambda k, r: (k * nr + r, 0))
    return pl.pallas_call(
        functools.partial(_s5_in_kernel, apply_norm=apply_norm),
        grid=(tc, nr),
        in_specs=[tok,
                  pl.BlockSpec((1, d), lambda k, r: (0, 0)),
                  pl.BlockSpec((None, d, d), lambda k, r: (layer, 0, 0), **_RESIDENT),
                  cast_in, cast_in],
        out_specs=[tok, cast_out, cast_out],
        out_shape=[jax.ShapeDtypeStruct((tc, m, d), F32),
                   jax.ShapeDtypeStruct((d, d), BF16), jax.ShapeDtypeStruct((d, d), BF16)],
        scratch_shapes=[pltpu.VMEM((d, d), BF16)],
        compiler_params=_params("arbitrary", "arbitrary"),
        name="s5_in",
    )(x, g, w, w_glu, w_out)


def _chunk_rows(u_ref):
    return jnp.concatenate([u_ref[k] for k in range(SSM_CHUNK)], axis=-1).astype(BF16)


def _s5_state_kernel(u_ref, m_ref, zre_ref, zim_ref):
    z = _dot(_chunk_rows(u_ref), m_ref[...])
    half = z.shape[1] // 2
    zre_ref[...] = z[:, :half]
    zim_ref[...] = z[:, half:]


def _s5_state_in(u, m_state):
    tc, m, d = u.shape
    nj, kk, ns = m_state.shape
    half = ns // 2
    return pl.pallas_call(
        _s5_state_kernel,
        grid=(nj,),
        in_specs=[pl.BlockSpec((tc, m, LANES), lambda j: (0, 0, j)),
                  pl.BlockSpec((None, kk, ns), lambda j: (j, 0, 0))],
        out_specs=[pl.BlockSpec((m, half), lambda j: (0, j)),
                   pl.BlockSpec((m, half), lambda j: (0, j))],
        out_shape=[jax.ShapeDtypeStruct((m, nj * half), F32)] * 2,
        compiler_params=_params("parallel"),
        name="s5_state_in",
    )(u, m_state)


def _s5_scan_kernel(zre_ref, zim_ref, are_ref, aim_ref, sre_ref, sim_ref, st_re, st_im):
    rows = zre_ref.shape[0]

    @pl.when(pl.program_id(1) == 0)
    def _():
        st_re[...] = jnp.zeros_like(st_re)
        st_im[...] = jnp.zeros_like(st_im)

    a_re = are_ref[...]
    a_im = aim_ref[...]

    def step(i, carry):
        s_re, s_im = carry
        row = pl.ds(i, 1)
        sre_ref[row, :] = s_re
        sim_ref[row, :] = s_im
        n_re = a_re * s_re - a_im * s_im + zre_ref[row, :]
        n_im = a_re * s_im + a_im * s_re + zim_ref[row, :]
        return n_re, n_im

    s_re, s_im = lax.fori_loop(0, rows, step, (st_re[...], st_im[...]))
    st_re[...] = s_re
    st_im[...] = s_im


def _s5_scan(z_re, z_im, a_re, a_im, rows_per_seq, tr):
    m, n = z_re.shape
    blk = pl.BlockSpec((tr, n), lambda s, r: (s * (rows_per_seq // tr) + r, 0))
    coef = pl.BlockSpec((1, n), lambda s, r: (0, 0))
    return pl.pallas_call(
        _s5_scan_kernel,
        grid=(m // rows_per_seq, rows_per_seq // tr),
        in_specs=[blk, blk, coef, coef],
        out_specs=[blk, blk],
        out_shape=[jax.ShapeDtypeStruct((m, n), F32)] * 2,
        scratch_shapes=[pltpu.VMEM((1, n), F32), pltpu.VMEM((1, n), F32)],
        compiler_params=_params("arbitrary", "arbitrary"),
        name="s5_scan",
    )(z_re, z_im, a_re, a_im)


def _s5_mix_kernel(u_ref, sre_ref, sim_ref, mi_ref, mc_ref, y_ref):
    y = _dot(_chunk_rows(u_ref), mi_ref[...])
    s = jnp.concatenate([sre_ref[...], sim_ref[...]], axis=-1).astype(BF16)
    y_ref[...] = y + _dot(s, mc_ref[...])


def _s5_mix(u, s_re, s_im, m_intra, m_carry):
    tc, m, d = u.shape
    nj, kk, _ = m_intra.shape
    ns = m_carry.shape[1]
    half = ns // 2
    return pl.pallas_call(
        _s5_mix_kernel,
        grid=(nj,),
        in_specs=[pl.BlockSpec((tc, m, LANES), lambda j: (0, 0, j)),
                  pl.BlockSpec((m, half), lambda j: (0, j)),
                  pl.BlockSpec((m, half), lambda j: (0, j)),
                  pl.BlockSpec((None, kk, kk), lambda j: (j, 0, 0)),
                  pl.BlockSpec((None, ns, kk), lambda j: (j, 0, 0))],
        out_specs=pl.BlockSpec((None, m, kk), lambda j: (j, 0, 0)),
        out_shape=jax.ShapeDtypeStruct((nj, m, kk), F32),
        compiler_params=_params("parallel"),
        name="s5_mix",
    )(u, s_re, s_im, m_intra, m_carry)


def _s5_out_kernel(y_ref, u_ref, h_ref, dsk_ref, wg_ref, wo_ref, g_ref, gn_ref, o_ref, hn_ref):
    nj = y_ref.shape[0]
    y = jnp.concatenate([y_ref[j] for j in range(nj)], axis=-1) + dsk_ref[...] * u_ref[...]
    z = jax.nn.gelu(y)
    gate = jax.nn.sigmoid(_dot(z.astype(BF16), wg_ref[...]))
    m = _dot((z * gate).astype(BF16), wo_ref[...])
    _finish(m, h_ref, g_ref, gn_ref, o_ref, hn_ref)


def _s5_out(y, u, h, d_skip, w_glu, w_out, g, g_next, tr):
    nj, m, _ = y.shape
    tc, _, d = h.shape
    tok = pl.BlockSpec((None, tr, d), lambda k, r: (k, r, 0))
    vec = pl.BlockSpec((1, d), lambda k, r: (0, 0))
    mat = pl.BlockSpec((d, d), lambda k, r: (0, 0), **_RESIDENT)
    return pl.pallas_call(
        _s5_out_kernel,
        grid=(tc, m // tr),
        in_specs=[pl.BlockSpec((nj, tr, LANES), lambda k, r: (0, r, k)),
                  tok, tok, vec, mat, mat, vec, vec],
        out_specs=[tok, tok],
        out_shape=[jax.ShapeDtypeStruct(h.shape, F32), jax.ShapeDtypeStruct(h.shape, BF16)],
        compiler_params=_params("parallel", "parallel"),
        name="s5_out",
    )(y, u, h, d_skip, w_glu, w_out, g, g_next)


def _s5_layer(h, hn, layer, g_pre, g_post, g_next, w_in, a_re, a_im, log_dt, b_re, b_im,
              c_re, c_im, d_skip, w_glu, w_out, rows_per_seq):
    m_intra, m_state, m_carry, pa_re, pa_im = _s5_tables(a_re, a_im, log_dt, b_re, b_im,
                                                         c_re, c_im)
    r = rows_per_seq
    u, w_glu, w_out = _s5_in(h if hn is None else hn, g_pre, w_in, w_glu, w_out, layer,
                             hn is None, _tile(r, 512))
    z_re, z_im = _s5_state_in(u, m_state)
    s_re, s_im = _s5_scan(z_re, z_im, pa_re, pa_im, r, _tile(r, 64))
    y = _s5_mix(u, s_re, s_im, m_intra, m_carry)
    return _s5_out(y, u, h, d_skip, w_glu, w_out, g_post, g_next, _tile(r, 256))


def _frames_by_offset(w, axes):
    cpb = SGU_BLOCK // SSM_CHUNK
    for ax in axes:
        shape = w.shape
        w = w.reshape(shape[:ax] + (cpb, SSM_CHUNK) + shape[ax + 1:])
        w = jnp.swapaxes(w, ax, ax + 1).reshape(shape)
    return w


def kernel(x, norm_g, s5_w_in, s5_a_re, s5_a_im, s5_log_dt, s5_b_re, s5_b_im, s5_c_re, s5_c_im,
           s5_d, s5_w_glu, s5_w_out, sgu_w_in, sgu_ln_g, sgu_ln_b, sgu_w_s, sgu_b_s, sgu_w_out,
           ffn_w_up, ffn_w_down):
    bsz, seq, d = x.shape
    t = bsz * seq
    tc = SSM_CHUNK
    r = seq // tc
    m = bsz * r
    depth = norm_g.shape[0]
    assert seq % SGU_BLOCK == 0 and d % LANES == 0 and SEQ_CHUNK % tc == 0
    gains = norm_g.reshape(depth, 4, 1, d)
    h = jnp.transpose(x.reshape(bsz, r, tc, d), (2, 0, 1, 3)).reshape(tc, m, d)
    hn = None
    for i in range(depth):
        g = gains[i]
        j = i // 2
        if i % 2 == 0:
            h, hn = _s5_layer(h, hn, j, g[0], g[1], g[2], s5_w_in, s5_a_re[j], s5_a_im[j],
                              s5_log_dt[j], s5_b_re[j], s5_b_im[j], s5_c_re[j], s5_c_im[j],
                              s5_d[j].reshape(1, d), s5_w_glu, s5_w_out, r)
        else:
            half = sgu_ln_g.shape[-1]
            heads = sgu_w_s.shape[1]
            z, s1, s2, w_out = _sgu_in(hn.reshape(t, d), sgu_w_in, sgu_w_out, j,
                                       _tile(t, 1024), _tile(half, 1024))
            h, hn = _sgu_down(z.reshape(tc, m, 2 * half), s1.reshape(tc, m, 1),
                              s2.reshape(tc, m, 1), sgu_ln_g[j].reshape(1, half),
                              sgu_ln_b[j].reshape(1, half), _frames_by_offset(sgu_w_s[j], (1, 2)),
                              _frames_by_offset(sgu_b_s[j], (1,)).reshape(heads, SGU_BLOCK, 1),
                              w_out, h, g[1], g[2], _tile(r, 64), 2)
        a, w_down = _up_proj(hn.reshape(t, d), ffn_w_up, ffn_w_down, i, _relu2,
                             _tile(t, 1024), _tile(ffn_w_up.shape[-1], 1024))
        last = i == depth - 1
        h, hn = _down_proj(a, w_down, h.reshape(t, d), g[3], g[3] if last else gains[i + 1][0],
                           not last, _tile(t, 512), _tile(d, 256))
        h = h.reshape(tc, m, d)
        hn = None if last else hn.reshape(tc, m, d)
    return jnp.transpose(h.reshape(tc, bsz, r, d), (1, 2, 0, 3)).reshape(bsz, seq, d)
```

```python
import functools

import jax
import jax.numpy as jnp
from jax import lax
from jax.experimental import pallas as pl
from jax.experimental.pallas import tpu as pltpu

F32 = jnp.float32
BF16 = jnp.bfloat16

EPS = 1e-6
EIG_CLIP = -1e-4
SEQ_CHUNK = 64
SGU_BLOCK = 128
S5_GROUP = 16
SSM_CHUNK = 8
LANES = 128
GROUPS_PER_TILE = LANES // S5_GROUP
VMEM_LIMIT = 56 * 1024 * 1024
VMEM_LIMIT_RESIDENT_W = 60 * 1024 * 1024

_dot = functools.partial(jnp.dot, preferred_element_type=F32)
_RESIDENT = dict(pipeline_mode=pl.Buffered(1))


def _params(*sem, vmem=VMEM_LIMIT):
    return pltpu.CompilerParams(dimension_semantics=sem, vmem_limit_bytes=vmem)


def _rms(x, g):
    ms = jnp.mean(x * x, axis=-1, keepdims=True)
    return x * lax.rsqrt(ms + EPS) * g


def _relu2(y):
    a = jnp.maximum(y, 0.0)
    return a * a


def _tile(n, pref):
    t = min(n, pref)
    assert n % t == 0, (n, t)
    return t


def _up_kernel(x_ref, w_ref, cw_ref, o_ref, cwo_ref, wb_ref, *, act):
    @pl.when(pl.program_id(1) == 0)
    def _():
        wb_ref[...] = w_ref[...].astype(BF16)

    o_ref[...] = act(_dot(x_ref[...], wb_ref[...])).astype(o_ref.dtype)
    cwo_ref[...] = cw_ref[...].astype(BF16)


def _cast_rows(cast_w, steps):
    rows = cast_w.shape[1] // steps
    assert rows * steps == cast_w.shape[1] and rows % 16 == 0
    return rows


def _up_proj(hn, w, cast_w, layer, act, tm, tn):
    t, d = hn.shape
    n = w.shape[2]
    nm = t // tm
    cr = _cast_rows(cast_w, (n // tn) * nm)
    cc = cast_w.shape[2]
    return pl.pallas_call(
        functools.partial(_up_kernel, act=act),
        grid=(n // tn, nm),
        in_specs=[pl.BlockSpec((tm, d), lambda j, i: (i, 0)),
                  pl.BlockSpec((None, d, tn), lambda j, i: (layer, 0, j)),
                  pl.BlockSpec((None, cr, cc), lambda j, i: (layer, j * nm + i, 0))],
        out_specs=[pl.BlockSpec((tm, tn), lambda j, i: (i, j)),
                   pl.BlockSpec((cr, cc), lambda j, i: (j * nm + i, 0))],
        out_shape=[jax.ShapeDtypeStruct((t, n), BF16),
                   jax.ShapeDtypeStruct(cast_w.shape[1:], BF16)],
        scratch_shapes=[pltpu.VMEM((d, tn), BF16)],
        compiler_params=_params("arbitrary", "arbitrary"),
        name="up_proj",
    )(hn, w, cast_w)


def _sgu_in_kernel(x_ref, w_ref, cw_ref, o_ref, s1_ref, s2_ref, cwo_ref, wb_ref, *, n_tiles):
    j = pl.program_id(0)
    i = pl.program_id(1)
    tm = x_ref.shape[0]

    @pl.when(i == 0)
    def _():
        wb_ref[...] = w_ref[...].astype(BF16)

    z = jax.nn.gelu(_dot(x_ref[...], wb_ref[...]))
    o_ref[...] = z.astype(o_ref.dtype)
    cwo_ref[...] = cw_ref[...].astype(BF16)
    rows = pl.ds(pl.multiple_of(i * tm, tm), tm)
    first_v = n_tiles // 2

    @pl.when(j == first_v)
    def _():
        s1_ref[rows, :] = jnp.sum(z, axis=-1, keepdims=True)
        s2_ref[rows, :] = jnp.sum(z * z, axis=-1, keepdims=True)

    @pl.when(j > first_v)
    def _():
        s1_ref[rows, :] += jnp.sum(z, axis=-1, keepdims=True)
        s2_ref[rows, :] += jnp.sum(z * z, axis=-1, keepdims=True)


def _sgu_in(hn, w, cast_w, layer, tm, tn):
    t, d = hn.shape
    n = w.shape[2]
    n_tiles = n // tn
    nm = t // tm
    assert n_tiles % 2 == 0
    cr = _cast_rows(cast_w, n_tiles * nm)
    cc = cast_w.shape[2]
    return pl.pallas_call(
        functools.partial(_sgu_in_kernel, n_tiles=n_tiles),
        grid=(n_tiles, nm),
        in_specs=[pl.BlockSpec((tm, d), lambda j, i: (i, 0)),
                  pl.BlockSpec((None, d, tn), lambda j, i: (layer, 0, j)),
                  pl.BlockSpec((None, cr, cc), lambda j, i: (layer, j * nm + i, 0))],
        out_specs=[pl.BlockSpec((tm, tn), lambda j, i: (i, j)),
                   pl.BlockSpec((t, 1), lambda j, i: (0, 0)),
                   pl.BlockSpec((t, 1), lambda j, i: (0, 0)),
                   pl.BlockSpec((cr, cc), lambda j, i: (j * nm + i, 0))],
        out_shape=[jax.ShapeDtypeStruct((t, n), BF16),
                   jax.ShapeDtypeStruct((t, 1), F32),
                   jax.ShapeDtypeStruct((t, 1), F32),
                   jax.ShapeDtypeStruct(cast_w.shape[1:], BF16)],
        scratch_shapes=[pltpu.VMEM((d, tn), BF16)],
        compiler_params=_params("arbitrary", "arbitrary"),
        name="sgu_in",
    )(hn, w, cast_w)


def _finish(f, h_ref, g_ref, gn_ref, o_ref, hn_ref):
    h = h_ref[...].reshape(f.shape) + _rms(f, g_ref[...])
    o_ref[...] = h.reshape(o_ref.shape)
    if hn_ref is not None:
        hn_ref[...] = _rms(h, gn_ref[...]).astype(BF16).reshape(hn_ref.shape)


def _down_kernel(a_ref, w_ref, h_ref, g_ref, gn_ref, o_ref, *rest, emit_hn):
    hn_ref = rest[0] if emit_hn else None
    _finish(_dot(a_ref[...], w_ref[...]), h_ref, g_ref, gn_ref, o_ref, hn_ref)


def _down_proj(a, w, h, g, g_next, emit_hn, tm):
    t, kk = a.shape
    d = w.shape[1]
    tok = pl.BlockSpec((tm, d), lambda i: (i, 0))
    vec = pl.BlockSpec((1, d), lambda i: (0, 0))
    out_specs = [tok]
    out_shape = [jax.ShapeDtypeStruct((t, d), F32)]
    if emit_hn:
        out_specs.append(tok)
        out_shape.append(jax.ShapeDtypeStruct((t, d), BF16))
    res = pl.pallas_call(
        functools.partial(_down_kernel, emit_hn=emit_hn),
        grid=(t // tm,),
        in_specs=[pl.BlockSpec((tm, kk), lambda i: (i, 0)),
                  pl.BlockSpec((kk, d), lambda i: (0, 0), **_RESIDENT),
                  tok, vec, vec],
        out_specs=out_specs,
        out_shape=out_shape,
        compiler_params=_params("parallel", vmem=VMEM_LIMIT_RESIDENT_W),
        name="down_proj",
    )(a, w, h, g, g_next)
    return res if emit_hn else (res[0], None)


def _sgu_down_kernel(u_ref, v_ref, s1_ref, s2_ref, lg_ref, lb_ref, ws_ref, bs_ref, w_ref,
                     h_ref, g_ref, gn_ref, o_ref, hn_ref, gate_ref, *, heads_per_group):
    tc, mt, half = u_ref.shape
    heads = ws_ref.shape[0]
    hd = half // heads
    cpb = SGU_BLOCK // tc
    cps = SEQ_CHUNK // tc

    row = lax.broadcasted_iota(jnp.int32, (SGU_BLOCK, SGU_BLOCK), 0)
    col = lax.broadcasted_iota(jnp.int32, (SGU_BLOCK, SGU_BLOCK), 1)
    causal = (col % cpb) // cps <= (row % cpb) // cps
    mean = s1_ref[...] * (1.0 / half)
    rstd = lax.rsqrt(s2_ref[...] * (1.0 / half) - mean * mean + EPS)

    def block_of(x, n):
        return jnp.concatenate([x[o, n * cpb:(n + 1) * cpb] for o in range(tc)], axis=0)

    f = None
    for hh in range(heads):
        cols = pl.ds(hh * hd, hd)
        w = jnp.where(causal, ws_ref[hh], 0.0).astype(BF16)
        lg = lg_ref[:, cols]
        lb = lb_ref[:, cols]
        bias = bs_ref[hh]
        for n in range(mt // cpb):
            rows = pl.ds(n * cpb, cpb)
            v = jnp.concatenate([v_ref[o, rows, cols] for o in range(tc)], axis=0).astype(F32)
            vn = ((v - block_of(mean, n)) * block_of(rstd, n) * lg + lb).astype(BF16)
            s = _dot(w, vn) + bias
            u = jnp.concatenate([u_ref[o, rows, cols] for o in range(tc)], axis=0).astype(F32)
            gated = (u * s).astype(BF16)
            for o in range(tc):
                gate_ref[o, rows, cols] = gated[o * cpb:(o + 1) * cpb]
        if (hh + 1) % heads_per_group == 0:
            gcols = pl.ds((hh + 1 - heads_per_group) * hd, heads_per_group * hd)
            part = _dot(gate_ref[:, :, gcols].reshape(tc * mt, heads_per_group * hd),
                        w_ref[gcols, :])
            f = part if f is None else f + part
    _finish(f, h_ref, g_ref, gn_ref, o_ref, hn_ref)


def _sgu_down(z, s1, s2, ln_g, ln_b, w_s, b_s, w_out, h, g, g_next, mt, heads_per_group):
    tc, m, n = z.shape
    half = n // 2
    heads = w_s.shape[0]
    assert heads % heads_per_group == 0
    d = w_out.shape[1]
    tok = lambda c: pl.BlockSpec((tc, mt, c), lambda i: (0, i, 0))
    const = lambda *shape: pl.BlockSpec(shape, lambda i: (0,) * len(shape), **_RESIDENT)
    return pl.pallas_call(
        functools.partial(_sgu_down_kernel, heads_per_group=heads_per_group),
        grid=(m // mt,),
        in_specs=[pl.BlockSpec((tc, mt, half), lambda i: (0, i, 0)),
                  pl.BlockSpec((tc, mt, half), lambda i: (0, i, 1)),
                  tok(1), tok(1),
                  const(1, half), const(1, half),
                  const(heads, SGU_BLOCK, SGU_BLOCK), const(heads, SGU_BLOCK, 1),
                  const(half, d),
                  tok(d), const(1, d), const(1, d)],
        out_specs=[tok(d), tok(d)],
        out_shape=[jax.ShapeDtypeStruct((tc, m, d), F32), jax.ShapeDtypeStruct((tc, m, d), BF16)],
        scratch_shapes=[pltpu.VMEM((tc, mt, half), BF16)],
        compiler_params=_params("parallel", vmem=VMEM_LIMIT_RESIDENT_W),
        name="sgu_down",
    )(z, z, s1, s2, ln_g, ln_b, w_s, b_s, w_out, h, g, g_next)


def _zoh(a_re, a_im, log_dt):
    lam_re = jnp.minimum(a_re, EIG_CLIP)
    lam_im = a_im
    dt = jnp.exp(log_dt)
    mag = jnp.exp(lam_re * dt)
    ab_re = mag * jnp.cos(lam_im * dt)
    ab_im = mag * jnp.sin(lam_im * dt)
    denom = lam_re * lam_re + lam_im * lam_im
    coef_re = ((ab_re - 1.0) * lam_re + ab_im * lam_im) / denom
    coef_im = (ab_im * lam_re - (ab_re - 1.0) * lam_im) / denom
    return ab_re, ab_im, coef_re, coef_im


def _cmul(x_re, x_im, y_re, y_im):
    return x_re * y_re - x_im * y_im, x_re * y_im + x_im * y_re


def _dot_3pass(a, b):
    a_hi = a.astype(BF16)
    a_lo = (a - a_hi.astype(F32)).astype(BF16)
    b_hi = b.astype(BF16)
    b_lo = (b - b_hi.astype(F32)).astype(BF16)
    return _dot(a_hi, b_hi) + _dot(a_hi, b_lo) + _dot(a_lo, b_hi)


def _s5_tables_kernel(are_ref, aim_ref, ldt_ref, bc1_ref, bc2_ref, cc_ref,
                      areb_ref, aimb_ref, ldtb_ref,
                      mi_ref, ms_ref, mc_ref, apr_ref, api_ref):
    tc, gl, gc, half = SSM_CHUNK, GROUPS_PER_TILE, S5_GROUP, LANES // 2
    lane = lax.broadcasted_iota(jnp.int32, (gc, LANES), 1)
    lower = lane < half
    sgn = jnp.where(lower, -1.0, 1.0)

    ab_re, ab_im, x_re, x_im = _zoh(are_ref[...], aim_ref[...], ldt_ref[...])
    bc1 = bc1_ref[...]
    bc2 = bc2_ref[...]
    ms_ref[...] = jnp.zeros(ms_ref.shape, ms_ref.dtype)
    w_lags = []
    for d in range(tc):
        w = x_re * bc1 + x_im * (sgn * bc2)
        wsw = x_re * bc2 - x_im * (sgn * bc1)
        w_lags.append(w)
        k = tc - 1 - d
        for g in range(gl):
            keep = lower if g % 2 == 0 else jnp.logical_not(lower)
            re_src, im_src = (w[g], wsw[g]) if g % 2 == 0 else (wsw[g], w[g])
            rows = pl.ds(k * LANES + g * gc, gc)
            ms_ref[rows, pl.ds((g // 2) * LANES, LANES)] = jnp.where(keep, re_src, 0.0).astype(BF16)
            ms_ref[rows, pl.ds((gl // 2 + g // 2) * LANES, LANES)] = (
                jnp.where(keep, im_src, 0.0).astype(BF16))
        x_re, x_im = _cmul(x_re, x_im, ab_re, ab_im)

    pw_re, pw_im = ab_re, ab_im
    for _ in range(tc - 1):
        pw_re, pw_im = _cmul(pw_re, pw_im, ab_re, ab_im)
    apr_ref[...] = pw_re
    api_ref[...] = pw_im

    row = lax.broadcasted_iota(jnp.int32, (LANES, LANES), 0)
    lane_sq = lax.broadcasted_iota(jnp.int32, (LANES, LANES), 1)
    c_sign = jnp.where(row < half, 1.0, -1.0)
    k_rows = []
    for g in range(gl):
        w_stack = jnp.concatenate([w_lags[d][g] for d in range(tc)], axis=0)
        k_g = _dot_3pass(w_stack, cc_ref[g] * c_sign)
        k_rows.append(jnp.where(lane_sq // gc == g, k_g, 0.0))
    zero = jnp.zeros((LANES, LANES), BF16)
    lag_blocks = [jnp.concatenate([k_rows[g][d * gc:(d + 1) * gc] for g in range(gl)],
                                  axis=0).astype(BF16) for d in range(tc)]
    for k in range(tc):
        for kp in range(tc):
            mi_ref[pl.ds(k * LANES, LANES), pl.ds(kp * LANES, LANES)] = (
                lag_blocks[kp - k] if kp >= k else zero)

    ab_re, ab_im, _, _ = _zoh(areb_ref[...], aimb_ref[...], ldtb_ref[...])
    c_re = cc_ref[:, :half, :]
    c_im = cc_ref[:, half:, :]
    grp = lax.broadcasted_iota(jnp.int32, (gl, half, LANES), 0)
    lane3 = lax.broadcasted_iota(jnp.int32, (gl, half, LANES), 2)
    diag = lane3 // gc == grp
    q_re, q_im = ab_re, ab_im
    for kp in range(tc):
        cm_re = jnp.where(diag, c_re * q_re - c_im * q_im, 0.0).astype(BF16)
        cm_im = jnp.where(diag, -(c_re * q_im + c_im * q_re), 0.0).astype(BF16)
        for g in range(gl):
            mc_ref[pl.ds(g * half, half), pl.ds(kp * LANES, LANES)] = cm_re[g]
            mc_ref[pl.ds((gl + g) * half, half), pl.ds(kp * LANES, LANES)] = cm_im[g]
        if kp + 1 < tc:
            q_re, q_im = _cmul(q_re, q_im, ab_re, ab_im)


def _s5_tables(a_re, a_im, log_dt, b_re, b_im, c_re, c_im):
    g, p = a_re.shape
    gc = b_re.shape[-1]
    gl = GROUPS_PER_TILE
    assert 2 * p == LANES and gc == S5_GROUP and g % gl == 0
    nj = g // gl
    kk = SSM_CHUNK * LANES
    ns = 2 * gl * p
    are_t = jnp.concatenate([a_re, a_re], axis=-1).reshape(g, 1, LANES)
    aim_t = jnp.concatenate([a_im, a_im], axis=-1).reshape(g, 1, LANES)
    ldt_t = jnp.broadcast_to(log_dt[:, None, None], (g, 1, LANES))
    bt_re = jnp.swapaxes(b_re, 1, 2)
    bt_im = jnp.swapaxes(b_im, 1, 2)
    bc1 = jnp.concatenate([bt_re, bt_im], axis=-1)
    bc2 = jnp.concatenate([bt_im, bt_re], axis=-1)
    ct = jnp.concatenate([jnp.swapaxes(c_re, 1, 2), jnp.swapaxes(c_im, 1, 2)], axis=1)
    cc = jnp.tile(ct, (1, 1, gl))
    lane_blk = lambda r, c: pl.BlockSpec((gl, r, c), lambda j: (j, 0, 0))
    mi, ms, mc, apr, api = pl.pallas_call(
        _s5_tables_kernel,
        grid=(nj,),
        in_specs=[lane_blk(1, LANES), lane_blk(1, LANES), lane_blk(1, LANES),
                  lane_blk(gc, LANES), lane_blk(gc, LANES), lane_blk(2 * p, LANES),
                  lane_blk(p, 1), lane_blk(p, 1), lane_blk(1, 1)],
        out_specs=[pl.BlockSpec((None, kk, kk), lambda j: (j, 0, 0)),
                   pl.BlockSpec((None, kk, ns), lambda j: (j, 0, 0)),
                   pl.BlockSpec((None, ns, kk), lambda j: (j, 0, 0)),
                   lane_blk(1, LANES), lane_blk(1, LANES)],
        out_shape=[jax.ShapeDtypeStruct((nj, kk, kk), BF16),
                   jax.ShapeDtypeStruct((nj, kk, ns), BF16),
                   jax.ShapeDtypeStruct((nj, ns, kk), BF16),
                   jax.ShapeDtypeStruct((g, 1, LANES), F32),
                   jax.ShapeDtypeStruct((g, 1, LANES), F32)],
        compiler_params=_params("parallel"),
        name="s5_tables",
    )(are_t, aim_t, ldt_t, bc1, bc2, cc, a_re[:, :, None], a_im[:, :, None],
      log_dt[:, None, None])
    pa_re = apr[:, 0, :p].reshape(1, g * p)
    pa_im = api[:, 0, :p].reshape(1, g * p)
    return mi, ms, mc, pa_re, pa_im


def _s5_in_kernel(x_ref, g_ref, w_ref, cg_ref, co_ref, u_ref, cgo_ref, coo_ref, wb_ref,
                  *, apply_norm):
    @pl.when((pl.program_id(0) == 0) & (pl.program_id(1) == 0))
    def _():
        wb_ref[...] = w_ref[...].astype(BF16)

    hn = _rms(x_ref[...], g_ref[...]).astype(BF16) if apply_norm else x_ref[...]
    u_ref[...] = _dot(hn, wb_ref[...])
    cgo_ref[...] = cg_ref[...].astype(BF16)
    coo_ref[...] = co_ref[...].astype(BF16)


def _s5_in(x, g, w, w_glu, w_out, layer, apply_norm, tr):
    tc, m, d = x.shape
    nr = m // tr
    cr = _cast_rows(w_glu, tc * nr)
    tok = pl.BlockSpec((None, tr, d), lambda k, r: (k, r, 0))
    cast_in = pl.BlockSpec((None, cr, d), lambda k, r: (layer, k * nr + r, 0))
    cast_out = pl.BlockSpec((cr, d), lambda k, r: (k * nr + r, 0))
    return pl.pallas_call(
        functools.partial(_s5_in_kernel, apply_norm=apply_norm),
        grid=(tc, nr),
        in_specs=[tok,
                  pl.BlockSpec((1, d), lambda k, r: (0, 0)),
                  pl.BlockSpec((None, d, d), lambda k, r: (layer, 0, 0), **_RESIDENT),
                  cast_in, cast_in],
        out_specs=[tok, cast_out, cast_out],
        out_shape=[jax.ShapeDtypeStruct((tc, m, d), F32),
                   jax.ShapeDtypeStruct((d, d), BF16), jax.ShapeDtypeStruct((d, d), BF16)],
        scratch_shapes=[pltpu.VMEM((d, d), BF16)],
        compiler_params=_params("arbitrary", "arbitrary"),
        name="s5_in",
    )(x, g, w, w_glu, w_out)


def _chunk_rows(u_ref):
    return jnp.concatenate([u_ref[k] for k in range(SSM_CHUNK)], axis=-1).astype(BF16)


def _s5_state_kernel(u_ref, m_ref, zre_ref, zim_ref):
    z = _dot(_chunk_rows(u_ref), m_ref[...])
    half = z.shape[1] // 2
    zre_ref[...] = z[:, :half]
    zim_ref[...] = z[:, half:]


def _s5_state_in(u, m_state):
    tc, m, d = u.shape
    nj, kk, ns = m_state.shape
    half = ns // 2
    return pl.pallas_call(
        _s5_state_kernel,
        grid=(nj,),
        in_specs=[pl.BlockSpec((tc, m, LANES), lambda j: (0, 0, j)),
                  pl.BlockSpec((None, kk, ns), lambda j: (j, 0, 0))],
        out_specs=[pl.BlockSpec((m, half), lambda j: (0, j)),
                   pl.BlockSpec((m, half), lambda j: (0, j))],
        out_shape=[jax.ShapeDtypeStruct((m, nj * half), F32)] * 2,
        compiler_params=_params("parallel"),
        name="s5_state_in",
    )(u, m_state)


def _s5_scan_kernel(zre_ref, zim_ref, are_ref, aim_ref, sre_ref, sim_ref, st_re, st_im):
    rows = zre_ref.shape[0]

    @pl.when(pl.program_id(1) == 0)
    def _():
        st_re[...] = jnp.zeros_like(st_re)
        st_im[...] = jnp.zeros_like(st_im)

    a_re = are_ref[...]
    a_im = aim_ref[...]

    def step(i, carry):
        s_re, s_im = carry
        row = pl.ds(i, 1)
        sre_ref[row, :] = s_re
        sim_ref[row, :] = s_im
        n_re = a_re * s_re - a_im * s_im + zre_ref[row, :]
        n_im = a_re * s_im + a_im * s_re + zim_ref[row, :]
        return n_re, n_im

    s_re, s_im = lax.fori_loop(0, rows, step, (st_re[...], st_im[...]))
    st_re[...] = s_re
    st_im[...] = s_im


def _s5_scan(z_re, z_im, a_re, a_im, rows_per_seq, tr):
    m, n = z_re.shape
    blk = pl.BlockSpec((tr, n), lambda s, r: (s * (rows_per_seq // tr) + r, 0))
    coef = pl.BlockSpec((1, n), lambda s, r: (0, 0))
    return pl.pallas_call(
        _s5_scan_kernel,
        grid=(m // rows_per_seq, rows_per_seq // tr),
        in_specs=[blk, blk, coef, coef],
        out_specs=[blk, blk],
        out_shape=[jax.ShapeDtypeStruct((m, n), F32)] * 2,
        scratch_shapes=[pltpu.VMEM((1, n), F32), pltpu.VMEM((1, n), F32)],
        compiler_params=_params("arbitrary", "arbitrary"),
        name="s5_scan",
    )(z_re, z_im, a_re, a_im)


def _s5_mix_kernel(u_ref, sre_ref, sim_ref, mi_ref, mc_ref, y_ref):
    y = _dot(_chunk_rows(u_ref), mi_ref[...])
    s = jnp.concatenate([sre_ref[...], sim_ref[...]], axis=-1).astype(BF16)
    y_ref[...] = y + _dot(s, mc_ref[...])


def _s5_mix(u, s_re, s_im, m_intra, m_carry):
    tc, m, d = u.shape
    nj, kk, _ = m_intra.shape
    ns = m_carry.shape[1]
    half = ns // 2
    return pl.pallas_call(
        _s5_mix_kernel,
        grid=(nj,),
        in_specs=[pl.BlockSpec((tc, m, LANES), lambda j: (0, 0, j)),
                  pl.BlockSpec((m, half), lambda j: (0, j)),
                  pl.BlockSpec((m, half), lambda j: (0, j)),
                  pl.BlockSpec((None, kk, kk), lambda j: (j, 0, 0)),
                  pl.BlockSpec((None, ns, kk), lambda j: (j, 0, 0))],
        out_specs=pl.BlockSpec((None, m, kk), lambda j: (j, 0, 0)),
        out_shape=jax.ShapeDtypeStruct((nj, m, kk), F32),
        compiler_params=_params("parallel"),
        name="s5_mix",
    )(u, s_re, s_im, m_intra, m_carry)


def _s5_out_kernel(y_ref, u_ref, h_ref, dsk_ref, wg_ref, wo_ref, g_ref, gn_ref, o_ref, hn_ref):
    nj = y_ref.shape[0]
    y = jnp.concatenate([y_ref[j] for j in range(nj)], axis=-1) + dsk_ref[...] * u_ref[...]
    z = jax.nn.gelu(y)
    gate = jax.nn.sigmoid(_dot(z.astype(BF16), wg_ref[...]))
    m = _dot((z * gate).astype(BF16), wo_ref[...])
    _finish(m, h_ref, g_ref, gn_ref, o_ref, hn_ref)


def _s5_out(y, u, h, d_skip, w_glu, w_out, g, g_next, tr):
    nj, m, _ = y.shape
    tc, _, d = h.shape
    tok = pl.BlockSpec((None, tr, d), lambda k, r: (k, r, 0))
    vec = pl.BlockSpec((1, d), lambda k, r: (0, 0))
    mat = pl.BlockSpec((d, d), lambda k, r: (0, 0), **_RESIDENT)
    return pl.pallas_call(
        _s5_out_kernel,
        grid=(tc, m // tr),
        in_specs=[pl.BlockSpec((nj, tr, LANES), lambda k, r: (0, r, k)),
                  tok, tok, vec, mat, mat, vec, vec],
        out_specs=[tok, tok],
        out_shape=[jax.ShapeDtypeStruct(h.shape, F32), jax.ShapeDtypeStruct(h.shape, BF16)],
        compiler_params=_params("parallel", "parallel"),
        name="s5_out",
    )(y, u, h, d_skip, w_glu, w_out, g, g_next)


def _s5_layer(h, hn, layer, g_pre, g_post, g_next, w_in, a_re, a_im, log_dt, b_re, b_im,
              c_re, c_im, d_skip, w_glu, w_out, rows_per_seq):
    m_intra, m_state, m_carry, pa_re, pa_im = _s5_tables(a_re, a_im, log_dt, b_re, b_im,
                                                         c_re, c_im)
    r = rows_per_seq
    u, w_glu, w_out = _s5_in(h if hn is None else hn, g_pre, w_in, w_glu, w_out, layer,
                             hn is None, _tile(r, 512))
    z_re, z_im = _s5_state_in(u, m_state)
    s_re, s_im = _s5_scan(z_re, z_im, pa_re, pa_im, r, _tile(r, 64))
    y = _s5_mix(u, s_re, s_im, m_intra, m_carry)
    return _s5_out(y, u, h, d_skip, w_glu, w_out, g_post, g_next, _tile(r, 256))


def _frames_by_offset(w, axes):
    cpb = SGU_BLOCK // SSM_CHUNK
    for ax in axes:
        shape = w.shape
        w = w.reshape(shape[:ax] + (cpb, SSM_CHUNK) + shape[ax + 1:])
        w = jnp.swapaxes(w, ax, ax + 1).reshape(shape)
    return w


def kernel(x, norm_g, s5_w_in, s5_a_re, s5_a_im, s5_log_dt, s5_b_re, s5_b_im, s5_c_re, s5_c_im,
           s5_d, s5_w_glu, s5_w_out, sgu_w_in, sgu_ln_g, sgu_ln_b, sgu_w_s, sgu_b_s, sgu_w_out,
           ffn_w_up, ffn_w_down):
    bsz, seq, d = x.shape
    t = bsz * seq
    tc = SSM_CHUNK
    r = seq // tc
    m = bsz * r
    depth = norm_g.shape[0]
    assert seq % SGU_BLOCK == 0 and d % LANES == 0 and SEQ_CHUNK % tc == 0
    gains = norm_g.reshape(depth, 4, 1, d)
    h = jnp.transpose(x.reshape(bsz, r, tc, d), (2, 0, 1, 3)).reshape(tc, m, d)
    hn = None
    for i in range(depth):
        g = gains[i]
        j = i // 2
        if i % 2 == 0:
            h, hn = _s5_layer(h, hn, j, g[0], g[1], g[2], s5_w_in, s5_a_re[j], s5_a_im[j],
                              s5_log_dt[j], s5_b_re[j], s5_b_im[j], s5_c_re[j], s5_c_im[j],
                              s5_d[j].reshape(1, d), s5_w_glu, s5_w_out, r)
        else:
            half = sgu_ln_g.shape[-1]
            heads = sgu_w_s.shape[1]
            z, s1, s2, w_out = _sgu_in(hn.reshape(t, d), sgu_w_in, sgu_w_out, j,
                                       _tile(t, 1024), _tile(half, 1024))
            h, hn = _sgu_down(z.reshape(tc, m, 2 * half), s1.reshape(tc, m, 1),
                              s2.reshape(tc, m, 1), sgu_ln_g[j].reshape(1, half),
                              sgu_ln_b[j].reshape(1, half), _frames_by_offset(sgu_w_s[j], (1, 2)),
                              _frames_by_offset(sgu_b_s[j], (1,)).reshape(heads, SGU_BLOCK, 1),
                              w_out, h, g[1], g[2], _tile(r, 32), _tile(heads, 4))
        a, w_down = _up_proj(hn.reshape(t, d), ffn_w_up, ffn_w_down, i, _relu2,
                             _tile(t, 1024), _tile(ffn_w_up.shape[-1], 1024))
        last = i == depth - 1
        h, hn = _down_proj(a, w_down, h.reshape(t, d), g[3], g[3] if last else gains[i + 1][0],
                           not last, _tile(t, 256))
        h = h.reshape(tc, m, d)
        hn = None if last else hn.reshape(tc, m, d)
    return jnp.transpose(h.reshape(tc, bsz, r, d), (1, 2, 0, 3)).reshape(bsz, seq, d)
```

```python
import functools

import jax
import jax.numpy as jnp
from jax import lax
from jax.experimental import pallas as pl
from jax.experimental.pallas import tpu as pltpu

F32 = jnp.float32
BF16 = jnp.bfloat16

EPS = 1e-6
EIG_CLIP = -1e-4
SEQ_CHUNK = 64
SGU_BLOCK = 128
S5_GROUP = 16
SSM_CHUNK = 8
LANES = 128
GROUPS_PER_TILE = LANES // S5_GROUP
VMEM_LIMIT = 56 * 1024 * 1024
VMEM_LIMIT_RESIDENT_W = 60 * 1024 * 1024

_dot = functools.partial(jnp.dot, preferred_element_type=F32)
_RESIDENT = dict(pipeline_mode=pl.Buffered(1))


def _params(*sem, vmem=VMEM_LIMIT):
    return pltpu.CompilerParams(dimension_semantics=sem, vmem_limit_bytes=vmem)


def _rms(x, g):
    ms = jnp.mean(x * x, axis=-1, keepdims=True)
    return x * lax.rsqrt(ms + EPS) * g


def _relu2(y):
    a = jnp.maximum(y, 0.0)
    return a * a


def _tile(n, pref):
    t = min(n, pref)
    assert n % t == 0, (n, t)
    return t


def _up_kernel(x_ref, w_ref, cw_ref, o_ref, cwo_ref, wb_ref, *, act, n_split):
    @pl.when(pl.program_id(1) == 0)
    def _():
        wb_ref[...] = w_ref[...].astype(BF16)

    x = x_ref[...]
    width = o_ref.shape[1] // n_split
    for s in range(n_split):
        cols = pl.ds(s * width, width)
        o_ref[:, cols] = act(_dot(x, wb_ref[:, cols])).astype(o_ref.dtype)
    cwo_ref[...] = cw_ref[...].astype(BF16)


def _cast_rows(cast_w, steps):
    rows = cast_w.shape[1] // steps
    assert rows * steps == cast_w.shape[1] and rows % 16 == 0
    return rows


def _up_proj(hn, w, cast_w, layer, act, tm, tn, n_split):
    t, d = hn.shape
    n = w.shape[2]
    nm = t // tm
    cr = _cast_rows(cast_w, (n // tn) * nm)
    cc = cast_w.shape[2]
    assert tn % (n_split * LANES) == 0
    return pl.pallas_call(
        functools.partial(_up_kernel, act=act, n_split=n_split),
        grid=(n // tn, nm),
        in_specs=[pl.BlockSpec((tm, d), lambda j, i: (i, 0)),
                  pl.BlockSpec((None, d, tn), lambda j, i: (layer, 0, j)),
                  pl.BlockSpec((None, cr, cc), lambda j, i: (layer, j * nm + i, 0))],
        out_specs=[pl.BlockSpec((tm, tn), lambda j, i: (i, j)),
                   pl.BlockSpec((cr, cc), lambda j, i: (j * nm + i, 0))],
        out_shape=[jax.ShapeDtypeStruct((t, n), BF16),
                   jax.ShapeDtypeStruct(cast_w.shape[1:], BF16)],
        scratch_shapes=[pltpu.VMEM((d, tn), BF16)],
        compiler_params=_params("arbitrary", "arbitrary"),
        name="up_proj",
    )(hn, w, cast_w)


def _finish(f, h_ref, g_ref, gn_ref, o_ref, hn_ref):
    h = h_ref[...].reshape(f.shape) + _rms(f, g_ref[...])
    o_ref[...] = h.reshape(o_ref.shape)
    if hn_ref is not None:
        hn_ref[...] = _rms(h, gn_ref[...]).astype(BF16).reshape(hn_ref.shape)


def _down_kernel(a_ref, w_ref, h_ref, g_ref, gn_ref, o_ref, *rest, emit_hn):
    hn_ref = rest[0] if emit_hn else None
    _finish(_dot(a_ref[...], w_ref[...]), h_ref, g_ref, gn_ref, o_ref, hn_ref)


def _down_proj(a, w, h, g, g_next, emit_hn, tm):
    t, kk = a.shape
    d = w.shape[1]
    tok = pl.BlockSpec((tm, d), lambda i: (i, 0))
    vec = pl.BlockSpec((1, d), lambda i: (0, 0))
    out_specs = [tok]
    out_shape = [jax.ShapeDtypeStruct((t, d), F32)]
    if emit_hn:
        out_specs.append(tok)
        out_shape.append(jax.ShapeDtypeStruct((t, d), BF16))
    res = pl.pallas_call(
        functools.partial(_down_kernel, emit_hn=emit_hn),
        grid=(t // tm,),
        in_specs=[pl.BlockSpec((tm, kk), lambda i: (i, 0)),
                  pl.BlockSpec((kk, d), lambda i: (0, 0), **_RESIDENT),
                  tok, vec, vec],
        out_specs=out_specs,
        out_shape=out_shape,
        compiler_params=_params("parallel", vmem=VMEM_LIMIT_RESIDENT_W),
        name="down_proj",
    )(a, w, h, g, g_next)
    return res if emit_hn else (res[0], None)


def _sgu_down_kernel(u_ref, v_ref, lg_ref, lb_ref, ws_ref, bs_ref, w_ref,
                     h_ref, g_ref, gn_ref, o_ref, hn_ref, gate_ref, *, heads_per_group):
    tc, mt, half = u_ref.shape
    heads = ws_ref.shape[0]
    hd = half // heads
    cpb = SGU_BLOCK // tc
    cps = SEQ_CHUNK // tc

    row = lax.broadcasted_iota(jnp.int32, (SGU_BLOCK, SGU_BLOCK), 0)
    col = lax.broadcasted_iota(jnp.int32, (SGU_BLOCK, SGU_BLOCK), 1)
    causal = (col % cpb) // cps <= (row % cpb) // cps
    s1 = jnp.zeros((tc, mt, LANES), F32)
    s2 = jnp.zeros((tc, mt, LANES), F32)
    for c in range(half // LANES):
        vc = v_ref[:, :, pl.ds(c * LANES, LANES)].astype(F32)
        s1 += vc
        s2 += vc * vc
    mean = jnp.sum(s1, axis=-1, keepdims=True) * (1.0 / half)
    rstd = lax.rsqrt(jnp.sum(s2, axis=-1, keepdims=True) * (1.0 / half) - mean * mean + EPS)

    def block_of(x, n):
        return jnp.concatenate([x[o, n * cpb:(n + 1) * cpb] for o in range(tc)], axis=0)

    f = None
    for hh in range(heads):
        cols = pl.ds(hh * hd, hd)
        w = jnp.where(causal, ws_ref[hh], 0.0).astype(BF16)
        lg = lg_ref[:, cols]
        lb = lb_ref[:, cols]
        bias = bs_ref[hh]
        for n in range(mt // cpb):
            rows = pl.ds(n * cpb, cpb)
            v = jnp.concatenate([v_ref[o, rows, cols] for o in range(tc)], axis=0).astype(F32)
            vn = ((v - block_of(mean, n)) * block_of(rstd, n) * lg + lb).astype(BF16)
            s = _dot(w, vn) + bias
            u = jnp.concatenate([u_ref[o, rows, cols] for o in range(tc)], axis=0).astype(F32)
            gated = (u * s).astype(BF16)
            for o in range(tc):
                gate_ref[o, rows, cols] = gated[o * cpb:(o + 1) * cpb]
        if (hh + 1) % heads_per_group == 0:
            gcols = pl.ds((hh + 1 - heads_per_group) * hd, heads_per_group * hd)
            part = _dot(gate_ref[:, :, gcols].reshape(tc * mt, heads_per_group * hd),
                        w_ref[gcols, :])
            f = part if f is None else f + part
    _finish(f, h_ref, g_ref, gn_ref, o_ref, hn_ref)


def _sgu_down(z, ln_g, ln_b, w_s, b_s, w_out, h, g, g_next, mt, heads_per_group):
    tc, m, n = z.shape
    half = n // 2
    heads = w_s.shape[0]
    assert heads % heads_per_group == 0
    d = w_out.shape[1]
    tok = lambda c: pl.BlockSpec((tc, mt, c), lambda i: (0, i, 0))
    const = lambda *shape: pl.BlockSpec(shape, lambda i: (0,) * len(shape), **_RESIDENT)
    return pl.pallas_call(
        functools.partial(_sgu_down_kernel, heads_per_group=heads_per_group),
        grid=(m // mt,),
        in_specs=[pl.BlockSpec((tc, mt, half), lambda i: (0, i, 0)),
                  pl.BlockSpec((tc, mt, half), lambda i: (0, i, 1)),
                  const(1, half), const(1, half),
                  const(heads, SGU_BLOCK, SGU_BLOCK), const(heads, SGU_BLOCK, 1),
                  const(half, d),
                  tok(d), const(1, d), const(1, d)],
        out_specs=[tok(d), tok(d)],
        out_shape=[jax.ShapeDtypeStruct((tc, m, d), F32), jax.ShapeDtypeStruct((tc, m, d), BF16)],
        scratch_shapes=[pltpu.VMEM((tc, mt, half), BF16)],
        compiler_params=_params("parallel", vmem=VMEM_LIMIT_RESIDENT_W),
        name="sgu_down",
    )(z, z, ln_g, ln_b, w_s, b_s, w_out, h, g, g_next)


def _zoh(a_re, a_im, log_dt):
    lam_re = jnp.minimum(a_re, EIG_CLIP)
    lam_im = a_im
    dt = jnp.exp(log_dt)
    mag = jnp.exp(lam_re * dt)
    ab_re = mag * jnp.cos(lam_im * dt)
    ab_im = mag * jnp.sin(lam_im * dt)
    denom = lam_re * lam_re + lam_im * lam_im
    coef_re = ((ab_re - 1.0) * lam_re + ab_im * lam_im) / denom
    coef_im = (ab_im * lam_re - (ab_re - 1.0) * lam_im) / denom
    return ab_re, ab_im, coef_re, coef_im


def _cmul(x_re, x_im, y_re, y_im):
    return x_re * y_re - x_im * y_im, x_re * y_im + x_im * y_re


def _dot_3pass(a, b):
    a_hi = a.astype(BF16)
    a_lo = (a - a_hi.astype(F32)).astype(BF16)
    b_hi = b.astype(BF16)
    b_lo = (b - b_hi.astype(F32)).astype(BF16)
    return _dot(a_hi, b_hi) + _dot(a_hi, b_lo) + _dot(a_lo, b_hi)


def _s5_tables_kernel(are_ref, aim_ref, ldt_ref, bc1_ref, bc2_ref, cc_ref,
                      areb_ref, aimb_ref, ldtb_ref,
                      mi_ref, ms_ref, mc_ref, apr_ref, api_ref):
    tc, gl, gc, half = SSM_CHUNK, GROUPS_PER_TILE, S5_GROUP, LANES // 2
    lane = lax.broadcasted_iota(jnp.int32, (gc, LANES), 1)
    lower = lane < half
    sgn = jnp.where(lower, -1.0, 1.0)

    ab_re, ab_im, x_re, x_im = _zoh(are_ref[...], aim_ref[...], ldt_ref[...])
    bc1 = bc1_ref[...]
    bc2 = bc2_ref[...]
    ms_ref[...] = jnp.zeros(ms_ref.shape, ms_ref.dtype)
    w_lags = []
    for d in range(tc):
        w = x_re * bc1 + x_im * (sgn * bc2)
        wsw = x_re * bc2 - x_im * (sgn * bc1)
        w_lags.append(w)
        k = tc - 1 - d
        for g in range(gl):
            keep = lower if g % 2 == 0 else jnp.logical_not(lower)
            re_src, im_src = (w[g], wsw[g]) if g % 2 == 0 else (wsw[g], w[g])
            rows = pl.ds(k * LANES + g * gc, gc)
            ms_ref[rows, pl.ds((g // 2) * LANES, LANES)] = jnp.where(keep, re_src, 0.0).astype(BF16)
            ms_ref[rows, pl.ds((gl // 2 + g // 2) * LANES, LANES)] = (
                jnp.where(keep, im_src, 0.0).astype(BF16))
        x_re, x_im = _cmul(x_re, x_im, ab_re, ab_im)

    pw_re, pw_im = ab_re, ab_im
    for _ in range(tc - 1):
        pw_re, pw_im = _cmul(pw_re, pw_im, ab_re, ab_im)
    apr_ref[...] = pw_re
    api_ref[...] = pw_im

    row = lax.broadcasted_iota(jnp.int32, (LANES, LANES), 0)
    lane_sq = lax.broadcasted_iota(jnp.int32, (LANES, LANES), 1)
    c_sign = jnp.where(row < half, 1.0, -1.0)
    k_rows = []
    for g in range(gl):
        w_stack = jnp.concatenate([w_lags[d][g] for d in range(tc)], axis=0)
        k_g = _dot_3pass(w_stack, cc_ref[g] * c_sign)
        k_rows.append(jnp.where(lane_sq // gc == g, k_g, 0.0))
    zero = jnp.zeros((LANES, LANES), BF16)
    lag_blocks = [jnp.concatenate([k_rows[g][d * gc:(d + 1) * gc] for g in range(gl)],
                                  axis=0).astype(BF16) for d in range(tc)]
    for k in range(tc):
        for kp in range(tc):
            mi_ref[pl.ds(k * LANES, LANES), pl.ds(kp * LANES, LANES)] = (
                lag_blocks[kp - k] if kp >= k else zero)

    ab_re, ab_im, _, _ = _zoh(areb_ref[...], aimb_ref[...], ldtb_ref[...])
    c_re = cc_ref[:, :half, :]
    c_im = cc_ref[:, half:, :]
    grp = lax.broadcasted_iota(jnp.int32, (gl, half, LANES), 0)
    lane3 = lax.broadcasted_iota(jnp.int32, (gl, half, LANES), 2)
    diag = lane3 // gc == grp
    q_re, q_im = ab_re, ab_im
    for kp in range(tc):
        cm_re = jnp.where(diag, c_re * q_re - c_im * q_im, 0.0).astype(BF16)
        cm_im = jnp.where(diag, -(c_re * q_im + c_im * q_re), 0.0).astype(BF16)
        for g in range(gl):
            mc_ref[pl.ds(g * half, half), pl.ds(kp * LANES, LANES)] = cm_re[g]
            mc_ref[pl.ds((gl + g) * half, half), pl.ds(kp * LANES, LANES)] = cm_im[g]
        if kp + 1 < tc:
            q_re, q_im = _cmul(q_re, q_im, ab_re, ab_im)


def _s5_tables(a_re, a_im, log_dt, b_re, b_im, c_re, c_im):
    g, p = a_re.shape
    gc = b_re.shape[-1]
    gl = GROUPS_PER_TILE
    assert 2 * p == LANES and gc == S5_GROUP and g % gl == 0
    nj = g // gl
    kk = SSM_CHUNK * LANES
    ns = 2 * gl * p
    are_t = jnp.concatenate([a_re, a_re], axis=-1).reshape(g, 1, LANES)
    aim_t = jnp.concatenate([a_im, a_im], axis=-1).reshape(g, 1, LANES)
    ldt_t = jnp.broadcast_to(log_dt[:, None, None], (g, 1, LANES))
    bt_re = jnp.swapaxes(b_re, 1, 2)
    bt_im = jnp.swapaxes(b_im, 1, 2)
    bc1 = jnp.concatenate([bt_re, bt_im], axis=-1)
    bc2 = jnp.concatenate([bt_im, bt_re], axis=-1)
    ct = jnp.concatenate([jnp.swapaxes(c_re, 1, 2), jnp.swapaxes(c_im, 1, 2)], axis=1)
    cc = jnp.tile(ct, (1, 1, gl))
    lane_blk = lambda r, c: pl.BlockSpec((gl, r, c), lambda j: (j, 0, 0))
    mi, ms, mc, apr, api = pl.pallas_call(
        _s5_tables_kernel,
        grid=(nj,),
        in_specs=[lane_blk(1, LANES), lane_blk(1, LANES), lane_blk(1, LANES),
                  lane_blk(gc, LANES), lane_blk(gc, LANES), lane_blk(2 * p, LANES),
                  lane_blk(p, 1), lane_blk(p, 1), lane_blk(1, 1)],
        out_specs=[pl.BlockSpec((None, kk, kk), lambda j: (j, 0, 0)),
                   pl.BlockSpec((None, kk, ns), lambda j: (j, 0, 0)),
                   pl.BlockSpec((None, ns, kk), lambda j: (j, 0, 0)),
                   lane_blk(1, LANES), lane_blk(1, LANES)],
        out_shape=[jax.ShapeDtypeStruct((nj, kk, kk), BF16),
                   jax.ShapeDtypeStruct((nj, kk, ns), BF16),
                   jax.ShapeDtypeStruct((nj, ns, kk), BF16),
                   jax.ShapeDtypeStruct((g, 1, LANES), F32),
                   jax.ShapeDtypeStruct((g, 1, LANES), F32)],
        compiler_params=_params("parallel"),
        name="s5_tables",
    )(are_t, aim_t, ldt_t, bc1, bc2, cc, a_re[:, :, None], a_im[:, :, None],
      log_dt[:, None, None])
    pa_re = apr[:, 0, :p].reshape(1, g * p)
    pa_im = api[:, 0, :p].reshape(1, g * p)
    return mi, ms, mc, pa_re, pa_im


def _s5_in_kernel(x_ref, g_ref, w_ref, cg_ref, co_ref, u_ref, cgo_ref, coo_ref, wb_ref,
                  *, apply_norm):
    @pl.when((pl.program_id(0) == 0) & (pl.program_id(1) == 0))
    def _():
        wb_ref[...] = w_ref[...].astype(BF16)

    hn = _rms(x_ref[...], g_ref[...]).astype(BF16) if apply_norm else x_ref[...]
    u_ref[...] = _dot(hn, wb_ref[...]).astype(u_ref.dtype)
    cgo_ref[...] = cg_ref[...].astype(BF16)
    coo_ref[...] = co_ref[...].astype(BF16)


def _s5_in(x, g, w, w_glu, w_out, layer, apply_norm, tr):
    tc, m, d = x.shape
    nr = m // tr
    cr = _cast_rows(w_glu, tc * nr)
    tok = pl.BlockSpec((None, tr, d), lambda k, r: (k, r, 0))
    cast_in = pl.BlockSpec((None, cr, d), lambda k, r: (layer, k * nr + r, 0))
    cast_out = pl.BlockSpec((cr, d), lambda k, r: (k * nr + r, 0))
    return pl.pallas_call(
        functools.partial(_s5_in_kernel, apply_norm=apply_norm),
        grid=(tc, nr),
        in_specs=[tok,
                  pl.BlockSpec((1, d), lambda k, r: (0, 0)),
                  pl.BlockSpec((None, d, d), lambda k, r: (layer, 0, 0), **_RESIDENT),
                  cast_in, cast_in],
        out_specs=[tok, cast_out, cast_out],
        out_shape=[jax.ShapeDtypeStruct((tc, m, d), BF16),
                   jax.ShapeDtypeStruct((d, d), BF16), jax.ShapeDtypeStruct((d, d), BF16)],
        scratch_shapes=[pltpu.VMEM((d, d), BF16)],
        compiler_params=_params("arbitrary", "arbitrary"),
        name="s5_in",
    )(x, g, w, w_glu, w_out)


def _chunk_rows(u_ref):
    return jnp.concatenate([u_ref[k] for k in range(SSM_CHUNK)], axis=-1)


def _s5_state_kernel(u_ref, m_ref, zre_ref, zim_ref):
    z = _dot(_chunk_rows(u_ref), m_ref[...])
    half = z.shape[1] // 2
    zre_ref[...] = z[:, :half]
    zim_ref[...] = z[:, half:]


def _s5_state_in(u, m_state):
    tc, m, d = u.shape
    nj, kk, ns = m_state.shape
    half = ns // 2
    return pl.pallas_call(
        _s5_state_kernel,
        grid=(nj,),
        in_specs=[pl.BlockSpec((tc, m, LANES), lambda j: (0, 0, j)),
                  pl.BlockSpec((None, kk, ns), lambda j: (j, 0, 0))],
        out_specs=[pl.BlockSpec((m, half), lambda j: (0, j)),
                   pl.BlockSpec((m, half), lambda j: (0, j))],
        out_shape=[jax.ShapeDtypeStruct((m, nj * half), F32)] * 2,
        compiler_params=_params("parallel"),
        name="s5_state_in",
    )(u, m_state)


def _s5_scan_kernel(zre_ref, zim_ref, are_ref, aim_ref, sre_ref, sim_ref, st_re, st_im):
    rows = zre_ref.shape[0]

    @pl.when(pl.program_id(1) == 0)
    def _():
        st_re[...] = jnp.zeros_like(st_re)
        st_im[...] = jnp.zeros_like(st_im)

    a_re = are_ref[...]
    a_im = aim_ref[...]

    def step(i, carry):
        s_re, s_im = carry
        row = pl.ds(i, 1)
        sre_ref[row, :] = s_re
        sim_ref[row, :] = s_im
        n_re = a_re * s_re - a_im * s_im + zre_ref[row, :]
        n_im = a_re * s_im + a_im * s_re + zim_ref[row, :]
        return n_re, n_im

    s_re, s_im = lax.fori_loop(0, rows, step, (st_re[...], st_im[...]))
    st_re[...] = s_re
    st_im[...] = s_im


def _s5_scan(z_re, z_im, a_re, a_im, rows_per_seq, tr):
    m, n = z_re.shape
    blk = pl.BlockSpec((tr, n), lambda s, r: (s * (rows_per_seq // tr) + r, 0))
    coef = pl.BlockSpec((1, n), lambda s, r: (0, 0))
    return pl.pallas_call(
        _s5_scan_kernel,
        grid=(m // rows_per_seq, rows_per_seq // tr),
        in_specs=[blk, blk, coef, coef],
        out_specs=[blk, blk],
        out_shape=[jax.ShapeDtypeStruct((m, n), F32)] * 2,
        scratch_shapes=[pltpu.VMEM((1, n), F32), pltpu.VMEM((1, n), F32)],
        compiler_params=_params("arbitrary", "arbitrary"),
        name="s5_scan",
    )(z_re, z_im, a_re, a_im)


def _s5_mix_kernel(u_ref, sre_ref, sim_ref, mi_ref, mc_ref, y_ref):
    x = _chunk_rows(u_ref)
    s = jnp.concatenate([sre_ref[...], sim_ref[...]], axis=-1).astype(BF16)
    kk = mi_ref.shape[0]
    step = 2 * LANES
    for c0 in range(0, kk, step):
        cols = pl.ds(c0, step)
        y = _dot(x[:, :c0 + step], mi_ref[pl.ds(0, c0 + step), cols]) + _dot(s, mc_ref[:, cols])
        y_ref[:, cols] = y.astype(y_ref.dtype)


def _s5_mix(u, s_re, s_im, m_intra, m_carry):
    tc, m, d = u.shape
    nj, kk, _ = m_intra.shape
    ns = m_carry.shape[1]
    half = ns // 2
    return pl.pallas_call(
        _s5_mix_kernel,
        grid=(nj,),
        in_specs=[pl.BlockSpec((tc, m, LANES), lambda j: (0, 0, j)),
                  pl.BlockSpec((m, half), lambda j: (0, j)),
                  pl.BlockSpec((m, half), lambda j: (0, j)),
                  pl.BlockSpec((None, kk, kk), lambda j: (j, 0, 0)),
                  pl.BlockSpec((None, ns, kk), lambda j: (j, 0, 0))],
        out_specs=pl.BlockSpec((None, m, kk), lambda j: (j, 0, 0)),
        out_shape=jax.ShapeDtypeStruct((nj, m, kk), BF16),
        compiler_params=_params("parallel"),
        name="s5_mix",
    )(u, s_re, s_im, m_intra, m_carry)


def _s5_out_kernel(y_ref, u_ref, h_ref, dsk_ref, wg_ref, wo_ref, g_ref, gn_ref, o_ref, hn_ref):
    nj = y_ref.shape[0]
    y = (jnp.concatenate([y_ref[j] for j in range(nj)], axis=-1).astype(F32)
         + dsk_ref[...] * u_ref[...].astype(F32))
    z = jax.nn.gelu(y)
    gate = jax.nn.sigmoid(_dot(z.astype(BF16), wg_ref[...]))
    m = _dot((z * gate).astype(BF16), wo_ref[...])
    _finish(m, h_ref, g_ref, gn_ref, o_ref, hn_ref)


def _s5_out(y, u, h, d_skip, w_glu, w_out, g, g_next, tr):
    nj, m, _ = y.shape
    tc, _, d = h.shape
    tok = pl.BlockSpec((None, tr, d), lambda k, r: (k, r, 0))
    vec = pl.BlockSpec((1, d), lambda k, r: (0, 0))
    mat = pl.BlockSpec((d, d), lambda k, r: (0, 0), **_RESIDENT)
    return pl.pallas_call(
        _s5_out_kernel,
        grid=(tc, m // tr),
        in_specs=[pl.BlockSpec((nj, tr, LANES), lambda k, r: (0, r, k)),
                  tok, tok, vec, mat, mat, vec, vec],
        out_specs=[tok, tok],
        out_shape=[jax.ShapeDtypeStruct(h.shape, F32), jax.ShapeDtypeStruct(h.shape, BF16)],
        compiler_params=_params("parallel", "parallel"),
        name="s5_out",
    )(y, u, h, d_skip, w_glu, w_out, g, g_next)


def _s5_layer(h, hn, layer, g_pre, g_post, g_next, w_in, a_re, a_im, log_dt, b_re, b_im,
              c_re, c_im, d_skip, w_glu, w_out, rows_per_seq):
    m_intra, m_state, m_carry, pa_re, pa_im = _s5_tables(a_re, a_im, log_dt, b_re, b_im,
                                                         c_re, c_im)
    r = rows_per_seq
    u, w_glu, w_out = _s5_in(h if hn is None else hn, g_pre, w_in, w_glu, w_out, layer,
                             hn is None, _tile(r, 512))
    z_re, z_im = _s5_state_in(u, m_state)
    s_re, s_im = _s5_scan(z_re, z_im, pa_re, pa_im, r, _tile(r, 64))
    y = _s5_mix(u, s_re, s_im, m_intra, m_carry)
    return _s5_out(y, u, h, d_skip, w_glu, w_out, g_post, g_next, _tile(r, 256))


def _frames_by_offset(w, axes):
    cpb = SGU_BLOCK // SSM_CHUNK
    for ax in axes:
        shape = w.shape
        w = w.reshape(shape[:ax] + (cpb, SSM_CHUNK) + shape[ax + 1:])
        w = jnp.swapaxes(w, ax, ax + 1).reshape(shape)
    return w


def kernel(x, norm_g, s5_w_in, s5_a_re, s5_a_im, s5_log_dt, s5_b_re, s5_b_im, s5_c_re, s5_c_im,
           s5_d, s5_w_glu, s5_w_out, sgu_w_in, sgu_ln_g, sgu_ln_b, sgu_w_s, sgu_b_s, sgu_w_out,
           ffn_w_up, ffn_w_down):
    bsz, seq, d = x.shape
    t = bsz * seq
    tc = SSM_CHUNK
    r = seq // tc
    m = bsz * r
    depth = norm_g.shape[0]
    assert seq % SGU_BLOCK == 0 and d % LANES == 0 and SEQ_CHUNK % tc == 0
    gains = norm_g.reshape(depth, 4, 1, d)
    h = jnp.transpose(x.reshape(bsz, r, tc, d), (2, 0, 1, 3)).reshape(tc, m, d)
    hn = None
    for i in range(depth):
        g = gains[i]
        j = i // 2
        if i % 2 == 0:
            h, hn = _s5_layer(h, hn, j, g[0], g[1], g[2], s5_w_in, s5_a_re[j], s5_a_im[j],
                              s5_log_dt[j], s5_b_re[j], s5_b_im[j], s5_c_re[j], s5_c_im[j],
                              s5_d[j].reshape(1, d), s5_w_glu, s5_w_out, r)
        else:
            half = sgu_ln_g.shape[-1]
            heads = sgu_w_s.shape[1]
            tn = _tile(half, 1024)
            z, w_out = _up_proj(hn.reshape(t, d), sgu_w_in, sgu_w_out, j, jax.nn.gelu,
                                _tile(t, 1024), tn, tn // 256 if tn % 256 == 0 else 1)
            h, hn = _sgu_down(z.reshape(tc, m, 2 * half), sgu_ln_g[j].reshape(1, half),
                              sgu_ln_b[j].reshape(1, half), _frames_by_offset(sgu_w_s[j], (1, 2)),
                              _frames_by_offset(sgu_b_s[j], (1,)).reshape(heads, SGU_BLOCK, 1),
                              w_out, h, g[1], g[2], _tile(r, 32), _tile(heads, 4))
        a, w_down = _up_proj(hn.reshape(t, d), ffn_w_up, ffn_w_down, i, _relu2,
                             _tile(t, 1024), _tile(ffn_w_up.shape[-1], 1024), 1)
        last = i == depth - 1
        h, hn = _down_proj(a, w_down, h.reshape(t, d), g[3], g[3] if last else gains[i + 1][0],
                           not last, _tile(t, 256))
        h = h.reshape(tc, m, d)
        hn = None if last else hn.reshape(tc, m, d)
    return jnp.transpose(h.reshape(tc, bsz, r, d), (1, 2, 0, 3)).reshape(bsz, seq, d)
```

```python
import functools

import jax
import jax.numpy as jnp
from jax import lax
from jax.experimental import pallas as pl
from jax.experimental.pallas import tpu as pltpu

F32 = jnp.float32
BF16 = jnp.bfloat16

EPS = 1e-6
EIG_CLIP = -1e-4
SEQ_CHUNK = 64
SGU_BLOCK = 128
S5_GROUP = 16
SSM_CHUNK = 8
LANES = 128
GROUPS_PER_TILE = LANES // S5_GROUP
VMEM_LIMIT = 56 * 1024 * 1024
VMEM_LIMIT_LARGE = 60 * 1024 * 1024

_dot = functools.partial(jnp.dot, preferred_element_type=F32)
_RESIDENT = dict(pipeline_mode=pl.Buffered(1))


def _params(*sem, vmem=VMEM_LIMIT):
    return pltpu.CompilerParams(dimension_semantics=sem, vmem_limit_bytes=vmem)


def _rms(x, g):
    ms = jnp.mean(x * x, axis=-1, keepdims=True)
    return x * lax.rsqrt(ms + EPS) * g


def _relu2(y):
    a = jnp.maximum(y, 0.0)
    return a * a


def _tile(n, pref):
    t = min(n, pref)
    assert n % t == 0, (n, t)
    return t


def _up_kernel(x_ref, w_ref, cw_ref, o_ref, cwo_ref, wb_ref, *, act, row_piece):
    @pl.when(pl.program_id(1) == 0)
    def _():
        wb_ref[...] = w_ref[...].astype(BF16)

    tm = x_ref.shape[0]
    for r0 in range(0, tm, row_piece):
        rows = pl.ds(r0, row_piece)
        o_ref[rows, :] = act(_dot(x_ref[rows, :], wb_ref[...])).astype(o_ref.dtype)
    cwo_ref[...] = cw_ref[...].astype(BF16)


def _cast_rows(cast_w, steps):
    rows = cast_w.shape[1] // steps
    assert rows * steps == cast_w.shape[1] and rows % 16 == 0
    return rows


def _up_proj(hn, w, cast_w, layer, act, tm, tn):
    t, d = hn.shape
    n = w.shape[2]
    nm = t // tm
    cr = _cast_rows(cast_w, (n // tn) * nm)
    cc = cast_w.shape[2]
    return pl.pallas_call(
        functools.partial(_up_kernel, act=act, row_piece=_tile(tm, 1024)),
        grid=(n // tn, nm),
        in_specs=[pl.BlockSpec((tm, d), lambda j, i: (i, 0)),
                  pl.BlockSpec((None, d, tn), lambda j, i: (layer, 0, j)),
                  pl.BlockSpec((None, cr, cc), lambda j, i: (layer, j * nm + i, 0))],
        out_specs=[pl.BlockSpec((tm, tn), lambda j, i: (i, j)),
                   pl.BlockSpec((cr, cc), lambda j, i: (j * nm + i, 0))],
        out_shape=[jax.ShapeDtypeStruct((t, n), BF16),
                   jax.ShapeDtypeStruct(cast_w.shape[1:], BF16)],
        scratch_shapes=[pltpu.VMEM((d, tn), BF16)],
        compiler_params=_params("arbitrary", "arbitrary", vmem=VMEM_LIMIT_LARGE),
        name="up_proj",
    )(hn, w, cast_w)


def _finish(f, h_ref, g_ref, gn_ref, o_ref, hn_ref, rows=None):
    if rows is None:
        h = h_ref[...].reshape(f.shape) + _rms(f, g_ref[...])
        o_ref[...] = h.reshape(o_ref.shape)
        if hn_ref is not None:
            hn_ref[...] = _rms(h, gn_ref[...]).astype(BF16).reshape(hn_ref.shape)
    else:
        h = h_ref[rows, :] + _rms(f, g_ref[...])
        o_ref[rows, :] = h
        if hn_ref is not None:
            hn_ref[rows, :] = _rms(h, gn_ref[...]).astype(BF16)


def _down_kernel(a_ref, w_ref, h_ref, g_ref, gn_ref, o_ref, *rest, emit_hn):
    hn_ref = rest[0] if emit_hn else None
    _finish(_dot(a_ref[...], w_ref[...]), h_ref, g_ref, gn_ref, o_ref, hn_ref)


def _down_proj(a, w, h, g, g_next, emit_hn, tm):
    t, kk = a.shape
    d = w.shape[1]
    tok = pl.BlockSpec((tm, d), lambda i: (i, 0))
    vec = pl.BlockSpec((1, d), lambda i: (0, 0))
    out_specs = [tok]
    out_shape = [jax.ShapeDtypeStruct((t, d), F32)]
    if emit_hn:
        out_specs.append(tok)
        out_shape.append(jax.ShapeDtypeStruct((t, d), BF16))
    res = pl.pallas_call(
        functools.partial(_down_kernel, emit_hn=emit_hn),
        grid=(t // tm,),
        in_specs=[pl.BlockSpec((tm, kk), lambda i: (i, 0)),
                  pl.BlockSpec((kk, d), lambda i: (0, 0), **_RESIDENT),
                  tok, vec, vec],
        out_specs=out_specs,
        out_shape=out_shape,
        compiler_params=_params("parallel", vmem=VMEM_LIMIT_LARGE),
        name="down_proj",
    )(a, w, h, g, g_next)
    return res if emit_hn else (res[0], None)


def _sgu_down_kernel(u_ref, v_ref, lg_ref, lb_ref, ws_ref, bs_ref, w_ref,
                     h_ref, g_ref, gn_ref, o_ref, hn_ref, gate_ref, *, heads_per_group):
    tc, mt, half = u_ref.shape
    heads = ws_ref.shape[0]
    hd = half // heads
    cpb = SGU_BLOCK // tc
    cps = SEQ_CHUNK // tc

    row = lax.broadcasted_iota(jnp.int32, (SGU_BLOCK, SGU_BLOCK), 0)
    col = lax.broadcasted_iota(jnp.int32, (SGU_BLOCK, SGU_BLOCK), 1)
    causal = (col % cpb) // cps <= (row % cpb) // cps
    s1 = jnp.zeros((tc, mt, LANES), F32)
    s2 = jnp.zeros((tc, mt, LANES), F32)
    for c in range(half // LANES):
        vc = v_ref[:, :, pl.ds(c * LANES, LANES)].astype(F32)
        s1 += vc
        s2 += vc * vc
    mean = jnp.sum(s1, axis=-1, keepdims=True) * (1.0 / half)
    rstd = lax.rsqrt(jnp.sum(s2, axis=-1, keepdims=True) * (1.0 / half) - mean * mean + EPS)

    def block_of(x, n):
        return jnp.concatenate([x[o, n * cpb:(n + 1) * cpb] for o in range(tc)], axis=0)

    f = None
    for hh in range(heads):
        cols = pl.ds(hh * hd, hd)
        w = jnp.where(causal, ws_ref[hh], 0.0).astype(BF16)
        lg = lg_ref[:, cols]
        lb = lb_ref[:, cols]
        bias = bs_ref[hh]
        for n in range(mt // cpb):
            rows = pl.ds(n * cpb, cpb)
            v = jnp.concatenate([v_ref[o, rows, cols] for o in range(tc)], axis=0).astype(F32)
            vn = ((v - block_of(mean, n)) * block_of(rstd, n) * lg + lb).astype(BF16)
            s = _dot(w, vn) + bias
            u = jnp.concatenate([u_ref[o, rows, cols] for o in range(tc)], axis=0).astype(F32)
            gated = (u * s).astype(BF16)
            for o in range(tc):
                gate_ref[o, rows, cols] = gated[o * cpb:(o + 1) * cpb]
        if (hh + 1) % heads_per_group == 0:
            gcols = pl.ds((hh + 1 - heads_per_group) * hd, heads_per_group * hd)
            part = _dot(gate_ref[:, :, gcols].reshape(tc * mt, heads_per_group * hd),
                        w_ref[gcols, :])
            f = part if f is None else f + part
    _finish(f, h_ref, g_ref, gn_ref, o_ref, hn_ref)


def _sgu_down(z, ln_g, ln_b, w_s, b_s, w_out, h, g, g_next, mt, heads_per_group):
    tc, m, n = z.shape
    half = n // 2
    heads = w_s.shape[0]
    assert heads % heads_per_group == 0
    d = w_out.shape[1]
    tok = lambda c: pl.BlockSpec((tc, mt, c), lambda i: (0, i, 0))
    const = lambda *shape: pl.BlockSpec(shape, lambda i: (0,) * len(shape), **_RESIDENT)
    return pl.pallas_call(
        functools.partial(_sgu_down_kernel, heads_per_group=heads_per_group),
        grid=(m // mt,),
        in_specs=[pl.BlockSpec((tc, mt, half), lambda i: (0, i, 0)),
                  pl.BlockSpec((tc, mt, half), lambda i: (0, i, 1)),
                  const(1, half), const(1, half),
                  const(heads, SGU_BLOCK, SGU_BLOCK), const(heads, SGU_BLOCK, 1),
                  const(half, d),
                  tok(d), const(1, d), const(1, d)],
        out_specs=[tok(d), tok(d)],
        out_shape=[jax.ShapeDtypeStruct((tc, m, d), F32), jax.ShapeDtypeStruct((tc, m, d), BF16)],
        scratch_shapes=[pltpu.VMEM((tc, mt, half), BF16)],
        compiler_params=_params("parallel", vmem=VMEM_LIMIT_LARGE),
        name="sgu_down",
    )(z, z, ln_g, ln_b, w_s, b_s, w_out, h, g, g_next)


def _zoh(a_re, a_im, log_dt):
    lam_re = jnp.minimum(a_re, EIG_CLIP)
    lam_im = a_im
    dt = jnp.exp(log_dt)
    mag = jnp.exp(lam_re * dt)
    ab_re = mag * jnp.cos(lam_im * dt)
    ab_im = mag * jnp.sin(lam_im * dt)
    denom = lam_re * lam_re + lam_im * lam_im
    coef_re = ((ab_re - 1.0) * lam_re + ab_im * lam_im) / denom
    coef_im = (ab_im * lam_re - (ab_re - 1.0) * lam_im) / denom
    return ab_re, ab_im, coef_re, coef_im


def _cmul(x_re, x_im, y_re, y_im):
    return x_re * y_re - x_im * y_im, x_re * y_im + x_im * y_re


def _dot_3pass(a, b):
    a_hi = a.astype(BF16)
    a_lo = (a - a_hi.astype(F32)).astype(BF16)
    b_hi = b.astype(BF16)
    b_lo = (b - b_hi.astype(F32)).astype(BF16)
    return _dot(a_hi, b_hi) + _dot(a_hi, b_lo) + _dot(a_lo, b_hi)


def _s5_tables_kernel(are_ref, aim_ref, ldt_ref, bc1_ref, bc2_ref, cc_ref,
                      areb_ref, aimb_ref, ldtb_ref,
                      mi_ref, ms_ref, mc_ref, apr_ref, api_ref):
    tc, gl, gc, half = SSM_CHUNK, GROUPS_PER_TILE, S5_GROUP, LANES // 2
    lane = lax.broadcasted_iota(jnp.int32, (gc, LANES), 1)
    lower = lane < half
    sgn = jnp.where(lower, -1.0, 1.0)

    ab_re, ab_im, x_re, x_im = _zoh(are_ref[...], aim_ref[...], ldt_ref[...])
    bc1 = bc1_ref[...]
    bc2 = bc2_ref[...]
    ms_ref[...] = jnp.zeros(ms_ref.shape, ms_ref.dtype)
    w_lags = []
    for d in range(tc):
        w = x_re * bc1 + x_im * (sgn * bc2)
        wsw = x_re * bc2 - x_im * (sgn * bc1)
        w_lags.append(w)
        k = tc - 1 - d
        for g in range(gl):
            keep = lower if g % 2 == 0 else jnp.logical_not(lower)
            re_src, im_src = (w[g], wsw[g]) if g % 2 == 0 else (wsw[g], w[g])
            rows = pl.ds(k * LANES + g * gc, gc)
            ms_ref[rows, pl.ds((g // 2) * LANES, LANES)] = jnp.where(keep, re_src, 0.0).astype(BF16)
            ms_ref[rows, pl.ds((gl // 2 + g // 2) * LANES, LANES)] = (
                jnp.where(keep, im_src, 0.0).astype(BF16))
        x_re, x_im = _cmul(x_re, x_im, ab_re, ab_im)

    pw_re, pw_im = ab_re, ab_im
    for _ in range(tc - 1):
        pw_re, pw_im = _cmul(pw_re, pw_im, ab_re, ab_im)
    apr_ref[...] = pw_re
    api_ref[...] = pw_im

    row = lax.broadcasted_iota(jnp.int32, (LANES, LANES), 0)
    lane_sq = lax.broadcasted_iota(jnp.int32, (LANES, LANES), 1)
    c_sign = jnp.where(row < half, 1.0, -1.0)
    k_rows = []
    for g in range(gl):
        w_stack = jnp.concatenate([w_lags[d][g] for d in range(tc)], axis=0)
        k_g = _dot_3pass(w_stack, cc_ref[g] * c_sign)
        k_rows.append(jnp.where(lane_sq // gc == g, k_g, 0.0))
    zero = jnp.zeros((LANES, LANES), BF16)
    lag_blocks = [jnp.concatenate([k_rows[g][d * gc:(d + 1) * gc] for g in range(gl)],
                                  axis=0).astype(BF16) for d in range(tc)]
    for k in range(tc):
        for kp in range(tc):
            mi_ref[pl.ds(k * LANES, LANES), pl.ds(kp * LANES, LANES)] = (
                lag_blocks[kp - k] if kp >= k else zero)

    ab_re, ab_im, _, _ = _zoh(areb_ref[...], aimb_ref[...], ldtb_ref[...])
    c_re = cc_ref[:, :half, :]
    c_im = cc_ref[:, half:, :]
    grp = lax.broadcasted_iota(jnp.int32, (gl, half, LANES), 0)
    lane3 = lax.broadcasted_iota(jnp.int32, (gl, half, LANES), 2)
    diag = lane3 // gc == grp
    q_re, q_im = ab_re, ab_im
    for kp in range(tc):
        cm_re = jnp.where(diag, c_re * q_re - c_im * q_im, 0.0).astype(BF16)
        cm_im = jnp.where(diag, -(c_re * q_im + c_im * q_re), 0.0).astype(BF16)
        for g in range(gl):
            mc_ref[pl.ds(g * half, half), pl.ds(kp * LANES, LANES)] = cm_re[g]
            mc_ref[pl.ds((gl + g) * half, half), pl.ds(kp * LANES, LANES)] = cm_im[g]
        if kp + 1 < tc:
            q_re, q_im = _cmul(q_re, q_im, ab_re, ab_im)


def _s5_tables(a_re, a_im, log_dt, b_re, b_im, c_re, c_im):
    g, p = a_re.shape
    gc = b_re.shape[-1]
    gl = GROUPS_PER_TILE
    assert 2 * p == LANES and gc == S5_GROUP and g % gl == 0
    nj = g // gl
    kk = SSM_CHUNK * LANES
    ns = 2 * gl * p
    are_t = jnp.concatenate([a_re, a_re], axis=-1).reshape(g, 1, LANES)
    aim_t = jnp.concatenate([a_im, a_im], axis=-1).reshape(g, 1, LANES)
    ldt_t = jnp.broadcast_to(log_dt[:, None, None], (g, 1, LANES))
    bt_re = jnp.swapaxes(b_re, 1, 2)
    bt_im = jnp.swapaxes(b_im, 1, 2)
    bc1 = jnp.concatenate([bt_re, bt_im], axis=-1)
    bc2 = jnp.concatenate([bt_im, bt_re], axis=-1)
    ct = jnp.concatenate([jnp.swapaxes(c_re, 1, 2), jnp.swapaxes(c_im, 1, 2)], axis=1)
    cc = jnp.tile(ct, (1, 1, gl))
    lane_blk = lambda r, c: pl.BlockSpec((gl, r, c), lambda j: (j, 0, 0))
    mi, ms, mc, apr, api = pl.pallas_call(
        _s5_tables_kernel,
        grid=(nj,),
        in_specs=[lane_blk(1, LANES), lane_blk(1, LANES), lane_blk(1, LANES),
                  lane_blk(gc, LANES), lane_blk(gc, LANES), lane_blk(2 * p, LANES),
                  lane_blk(p, 1), lane_blk(p, 1), lane_blk(1, 1)],
        out_specs=[pl.BlockSpec((None, kk, kk), lambda j: (j, 0, 0)),
                   pl.BlockSpec((None, kk, ns), lambda j: (j, 0, 0)),
                   pl.BlockSpec((None, ns, kk), lambda j: (j, 0, 0)),
                   lane_blk(1, LANES), lane_blk(1, LANES)],
        out_shape=[jax.ShapeDtypeStruct((nj, kk, kk), BF16),
                   jax.ShapeDtypeStruct((nj, kk, ns), BF16),
                   jax.ShapeDtypeStruct((nj, ns, kk), BF16),
                   jax.ShapeDtypeStruct((g, 1, LANES), F32),
                   jax.ShapeDtypeStruct((g, 1, LANES), F32)],
        compiler_params=_params("parallel"),
        name="s5_tables",
    )(are_t, aim_t, ldt_t, bc1, bc2, cc, a_re[:, :, None], a_im[:, :, None],
      log_dt[:, None, None])
    pa_re = apr[:, 0, :p].reshape(1, g * p)
    pa_im = api[:, 0, :p].reshape(1, g * p)
    return mi, ms, mc, pa_re, pa_im


def _s5_in_kernel(x_ref, g_ref, w_ref, cg_ref, co_ref, u_ref, cgo_ref, coo_ref, wb_ref,
                  *, apply_norm):
    @pl.when((pl.program_id(0) == 0) & (pl.program_id(1) == 0))
    def _():
        wb_ref[...] = w_ref[...].astype(BF16)

    hn = _rms(x_ref[...], g_ref[...]).astype(BF16) if apply_norm else x_ref[...]
    u_ref[...] = _dot(hn, wb_ref[...]).astype(u_ref.dtype)
    cgo_ref[...] = cg_ref[...].astype(BF16)
    coo_ref[...] = co_ref[...].astype(BF16)


def _s5_in(x, g, w, w_glu, w_out, layer, apply_norm, tr):
    tc, m, d = x.shape
    nr = m // tr
    cr = _cast_rows(w_glu, tc * nr)
    tok = pl.BlockSpec((None, tr, d), lambda k, r: (k, r, 0))
    cast_in = pl.BlockSpec((None, cr, d), lambda k, r: (layer, k * nr + r, 0))
    cast_out = pl.BlockSpec((cr, d), lambda k, r: (k * nr + r, 0))
    return pl.pallas_call(
        functools.partial(_s5_in_kernel, apply_norm=apply_norm),
        grid=(tc, nr),
        in_specs=[tok,
                  pl.BlockSpec((1, d), lambda k, r: (0, 0)),
                  pl.BlockSpec((None, d, d), lambda k, r: (layer, 0, 0), **_RESIDENT),
                  cast_in, cast_in],
        out_specs=[tok, cast_out, cast_out],
        out_shape=[jax.ShapeDtypeStruct((tc, m, d), BF16),
                   jax.ShapeDtypeStruct((d, d), BF16), jax.ShapeDtypeStruct((d, d), BF16)],
        scratch_shapes=[pltpu.VMEM((d, d), BF16)],
        compiler_params=_params("arbitrary", "arbitrary"),
        name="s5_in",
    )(x, g, w, w_glu, w_out)


def _chunk_rows(u_ref):
    return jnp.concatenate([u_ref[k] for k in range(SSM_CHUNK)], axis=-1)


def _s5_state_kernel(u_ref, m_ref, zre_ref, zim_ref):
    z = _dot(_chunk_rows(u_ref), m_ref[...])
    half = z.shape[1] // 2
    zre_ref[...] = z[:, :half]
    zim_ref[...] = z[:, half:]


def _s5_state_in(u, m_state):
    tc, m, d = u.shape
    nj, kk, ns = m_state.shape
    half = ns // 2
    return pl.pallas_call(
        _s5_state_kernel,
        grid=(nj,),
        in_specs=[pl.BlockSpec((tc, m, LANES), lambda j: (0, 0, j)),
                  pl.BlockSpec((None, kk, ns), lambda j: (j, 0, 0))],
        out_specs=[pl.BlockSpec((m, half), lambda j: (0, j)),
                   pl.BlockSpec((m, half), lambda j: (0, j))],
        out_shape=[jax.ShapeDtypeStruct((m, nj * half), F32)] * 2,
        compiler_params=_params("parallel"),
        name="s5_state_in",
    )(u, m_state)


def _s5_scan_kernel(zre_ref, zim_ref, are_ref, aim_ref, sre_ref, sim_ref, st_re, st_im):
    rows = zre_ref.shape[0]

    @pl.when(pl.program_id(1) == 0)
    def _():
        st_re[...] = jnp.zeros_like(st_re)
        st_im[...] = jnp.zeros_like(st_im)

    a_re = are_ref[...]
    a_im = aim_ref[...]

    def step(i, carry):
        s_re, s_im = carry
        row = pl.ds(i, 1)
        sre_ref[row, :] = s_re
        sim_ref[row, :] = s_im
        n_re = a_re * s_re - a_im * s_im + zre_ref[row, :]
        n_im = a_re * s_im + a_im * s_re + zim_ref[row, :]
        return n_re, n_im

    s_re, s_im = lax.fori_loop(0, rows, step, (st_re[...], st_im[...]))
    st_re[...] = s_re
    st_im[...] = s_im


def _s5_scan(z_re, z_im, a_re, a_im, rows_per_seq, tr):
    m, n = z_re.shape
    blk = pl.BlockSpec((tr, n), lambda s, r: (s * (rows_per_seq // tr) + r, 0))
    coef = pl.BlockSpec((1, n), lambda s, r: (0, 0))
    return pl.pallas_call(
        _s5_scan_kernel,
        grid=(m // rows_per_seq, rows_per_seq // tr),
        in_specs=[blk, blk, coef, coef],
        out_specs=[blk, blk],
        out_shape=[jax.ShapeDtypeStruct((m, n), F32)] * 2,
        scratch_shapes=[pltpu.VMEM((1, n), F32), pltpu.VMEM((1, n), F32)],
        compiler_params=_params("arbitrary", "arbitrary"),
        name="s5_scan",
    )(z_re, z_im, a_re, a_im)


def _s5_mix_kernel(u_ref, sre_ref, sim_ref, mi_ref, mc_ref, y_ref):
    x = _chunk_rows(u_ref)
    s = jnp.concatenate([sre_ref[...], sim_ref[...]], axis=-1).astype(BF16)
    kk = mi_ref.shape[0]
    step = 2 * LANES
    for c0 in range(0, kk, step):
        cols = pl.ds(c0, step)
        y = _dot(x[:, :c0 + step], mi_ref[pl.ds(0, c0 + step), cols]) + _dot(s, mc_ref[:, cols])
        y_ref[:, cols] = y.astype(y_ref.dtype)


def _s5_mix(u, s_re, s_im, m_intra, m_carry):
    tc, m, d = u.shape
    nj, kk, _ = m_intra.shape
    ns = m_carry.shape[1]
    half = ns // 2
    return pl.pallas_call(
        _s5_mix_kernel,
        grid=(nj,),
        in_specs=[pl.BlockSpec((tc, m, LANES), lambda j: (0, 0, j)),
                  pl.BlockSpec((m, half), lambda j: (0, j)),
                  pl.BlockSpec((m, half), lambda j: (0, j)),
                  pl.BlockSpec((None, kk, kk), lambda j: (j, 0, 0)),
                  pl.BlockSpec((None, ns, kk), lambda j: (j, 0, 0))],
        out_specs=pl.BlockSpec((None, m, kk), lambda j: (j, 0, 0)),
        out_shape=jax.ShapeDtypeStruct((nj, m, kk), BF16),
        compiler_params=_params("parallel"),
        name="s5_mix",
    )(u, s_re, s_im, m_intra, m_carry)


def _s5_out_kernel(y_ref, u_ref, h_ref, dsk_ref, wg_ref, wo_ref, g_ref, gn_ref, o_ref, hn_ref,
                   *, row_piece):
    nj, tr, _ = y_ref.shape
    for r0 in range(0, tr, row_piece):
        rows = pl.ds(r0, row_piece)
        y = (jnp.concatenate([y_ref[j, rows, :] for j in range(nj)], axis=-1).astype(F32)
             + dsk_ref[...] * u_ref[rows, :].astype(F32))
        z = jax.nn.gelu(y)
        gate = jax.nn.sigmoid(_dot(z.astype(BF16), wg_ref[...]))
        m = _dot((z * gate).astype(BF16), wo_ref[...])
        _finish(m, h_ref, g_ref, gn_ref, o_ref, hn_ref, rows)


def _s5_out(y, u, h, d_skip, w_glu, w_out, g, g_next, tr):
    nj, m, _ = y.shape
    tc, _, d = h.shape
    tok = pl.BlockSpec((None, tr, d), lambda k, r: (k, r, 0))
    vec = pl.BlockSpec((1, d), lambda k, r: (0, 0))
    mat = pl.BlockSpec((d, d), lambda k, r: (0, 0), **_RESIDENT)
    return pl.pallas_call(
        functools.partial(_s5_out_kernel, row_piece=_tile(tr, 256)),
        grid=(tc, m // tr),
        in_specs=[pl.BlockSpec((nj, tr, LANES), lambda k, r: (0, r, k)),
                  tok, tok, vec, mat, mat, vec, vec],
        out_specs=[tok, tok],
        out_shape=[jax.ShapeDtypeStruct(h.shape, F32), jax.ShapeDtypeStruct(h.shape, BF16)],
        compiler_params=_params("parallel", "parallel"),
        name="s5_out",
    )(y, u, h, d_skip, w_glu, w_out, g, g_next)


def _s5_layer(h, hn, layer, g_pre, g_post, g_next, w_in, a_re, a_im, log_dt, b_re, b_im,
              c_re, c_im, d_skip, w_glu, w_out, rows_per_seq):
    m_intra, m_state, m_carry, pa_re, pa_im = _s5_tables(a_re, a_im, log_dt, b_re, b_im,
                                                         c_re, c_im)
    r = rows_per_seq
    u, w_glu, w_out = _s5_in(h if hn is None else hn, g_pre, w_in, w_glu, w_out, layer,
                             hn is None, _tile(r, 512))
    z_re, z_im = _s5_state_in(u, m_state)
    s_re, s_im = _s5_scan(z_re, z_im, pa_re, pa_im, r, _tile(r, 64))
    y = _s5_mix(u, s_re, s_im, m_intra, m_carry)
    return _s5_out(y, u, h, d_skip, w_glu, w_out, g_post, g_next, _tile(r, 512))


def _frames_by_offset(w, axes):
    cpb = SGU_BLOCK // SSM_CHUNK
    for ax in axes:
        shape = w.shape
        w = w.reshape(shape[:ax] + (cpb, SSM_CHUNK) + shape[ax + 1:])
        w = jnp.swapaxes(w, ax, ax + 1).reshape(shape)
    return w


def kernel(x, norm_g, s5_w_in, s5_a_re, s5_a_im, s5_log_dt, s5_b_re, s5_b_im, s5_c_re, s5_c_im,
           s5_d, s5_w_glu, s5_w_out, sgu_w_in, sgu_ln_g, sgu_ln_b, sgu_w_s, sgu_b_s, sgu_w_out,
           ffn_w_up, ffn_w_down):
    bsz, seq, d = x.shape
    t = bsz * seq
    tc = SSM_CHUNK
    r = seq // tc
    m = bsz * r
    depth = norm_g.shape[0]
    assert seq % SGU_BLOCK == 0 and d % LANES == 0 and SEQ_CHUNK % tc == 0
    gains = norm_g.reshape(depth, 4, 1, d)
    h = jnp.transpose(x.reshape(bsz, r, tc, d), (2, 0, 1, 3)).reshape(tc, m, d)
    hn = None
    for i in range(depth):
        g = gains[i]
        j = i // 2
        if i % 2 == 0:
            h, hn = _s5_layer(h, hn, j, g[0], g[1], g[2], s5_w_in, s5_a_re[j], s5_a_im[j],
                              s5_log_dt[j], s5_b_re[j], s5_b_im[j], s5_c_re[j], s5_c_im[j],
                              s5_d[j].reshape(1, d), s5_w_glu, s5_w_out, r)
        else:
            half = sgu_ln_g.shape[-1]
            heads = sgu_w_s.shape[1]
            z, w_out = _up_proj(hn.reshape(t, d), sgu_w_in, sgu_w_out, j, jax.nn.gelu,
                                _tile(t, 2048), _tile(half, 1024))
            h, hn = _sgu_down(z.reshape(tc, m, 2 * half), sgu_ln_g[j].reshape(1, half),
                              sgu_ln_b[j].reshape(1, half), _frames_by_offset(sgu_w_s[j], (1, 2)),
                              _frames_by_offset(sgu_b_s[j], (1,)).reshape(heads, SGU_BLOCK, 1),
                              w_out, h, g[1], g[2], _tile(r, 32), _tile(heads, 4))
        a, w_down = _up_proj(hn.reshape(t, d), ffn_w_up, ffn_w_down, i, _relu2,
                             _tile(t, 2048), _tile(ffn_w_up.shape[-1], 1024))
        last = i == depth - 1
        h, hn = _down_proj(a, w_down, h.reshape(t, d), g[3], g[3] if last else gains[i + 1][0],
                           not last, _tile(t, 256))
        h = h.reshape(tc, m, d)
        hn = None if last else hn.reshape(tc, m, d)
    return jnp.transpose(h.reshape(tc, bsz, r, d), (1, 2, 0, 3)).reshape(bsz, seq, d)
```

```python
import functools
import math

import jax
import jax.numpy as jnp
from jax import lax
from jax.experimental import pallas as pl
from jax.experimental.pallas import tpu as pltpu

F32 = jnp.float32
BF16 = jnp.bfloat16

EPS = 1e-6
EIG_CLIP = -1e-4
SEQ_CHUNK = 64
SGU_BLOCK = 128
S5_GROUP = 16
SSM_CHUNK = 8
LANES = 128
GROUPS_PER_TILE = LANES // S5_GROUP
VMEM_LIMIT = 56 * 1024 * 1024
VMEM_LIMIT_LARGE = 60 * 1024 * 1024

_dot = functools.partial(jnp.dot, preferred_element_type=F32)
_RESIDENT = dict(pipeline_mode=pl.Buffered(1))


def _params(*sem, vmem=VMEM_LIMIT):
    return pltpu.CompilerParams(dimension_semantics=sem, vmem_limit_bytes=vmem)


def _rms(x, g):
    ms = jnp.mean(x * x, axis=-1, keepdims=True)
    return x * lax.rsqrt(ms + EPS) * g


def _relu2(y):
    a = jnp.maximum(y, 0.0)
    return a * a


_GELU_C1 = math.sqrt(2.0 / math.pi)
_GELU_C2 = _GELU_C1 * 0.044715


def _gelu(x):
    hx = 0.5 * x
    return hx + hx * jnp.tanh(x * (_GELU_C1 + _GELU_C2 * (x * x)))


def _tile(n, pref):
    t = min(n, pref)
    assert n % t == 0, (n, t)
    return t


def _up_kernel(x_ref, w_ref, cw_ref, o_ref, cwo_ref, wb_ref, *, act, row_piece):
    @pl.when(pl.program_id(1) == 0)
    def _():
        wb_ref[...] = w_ref[...].astype(BF16)

    tm = x_ref.shape[0]
    for r0 in range(0, tm, row_piece):
        rows = pl.ds(r0, row_piece)
        o_ref[rows, :] = act(_dot(x_ref[rows, :], wb_ref[...])).astype(o_ref.dtype)
    cwo_ref[...] = cw_ref[...].astype(BF16)


def _cast_rows(cast_w, steps):
    rows = cast_w.shape[1] // steps
    assert rows * steps == cast_w.shape[1] and rows % 16 == 0
    return rows


def _up_proj(hn, w, cast_w, layer, act, tm, tn):
    t, d = hn.shape
    n = w.shape[2]
    nm = t // tm
    cr = _cast_rows(cast_w, (n // tn) * nm)
    cc = cast_w.shape[2]
    return pl.pallas_call(
        functools.partial(_up_kernel, act=act, row_piece=_tile(tm, 1024)),
        grid=(n // tn, nm),
        in_specs=[pl.BlockSpec((tm, d), lambda j, i: (i, 0)),
                  pl.BlockSpec((None, d, tn), lambda j, i: (layer, 0, j)),
                  pl.BlockSpec((None, cr, cc), lambda j, i: (layer, j * nm + i, 0))],
        out_specs=[pl.BlockSpec((tm, tn), lambda j, i: (i, j)),
                   pl.BlockSpec((cr, cc), lambda j, i: (j * nm + i, 0))],
        out_shape=[jax.ShapeDtypeStruct((t, n), BF16),
                   jax.ShapeDtypeStruct(cast_w.shape[1:], BF16)],
        scratch_shapes=[pltpu.VMEM((d, tn), BF16)],
        compiler_params=_params("arbitrary", "arbitrary", vmem=VMEM_LIMIT_LARGE),
        name="up_proj",
    )(hn, w, cast_w)


def _finish(f, h_ref, g_ref, gn_ref, o_ref, hn_ref, rows=None):
    if rows is None:
        h = h_ref[...].reshape(f.shape) + _rms(f, g_ref[...])
        o_ref[...] = h.reshape(o_ref.shape)
        if hn_ref is not None:
            hn_ref[...] = _rms(h, gn_ref[...]).astype(BF16).reshape(hn_ref.shape)
    else:
        h = h_ref[rows, :] + _rms(f, g_ref[...])
        o_ref[rows, :] = h
        if hn_ref is not None:
            hn_ref[rows, :] = _rms(h, gn_ref[...]).astype(BF16)


def _down_kernel(a_ref, w_ref, h_ref, g_ref, gn_ref, o_ref, *rest, emit_hn):
    hn_ref = rest[0] if emit_hn else None
    _finish(_dot(a_ref[...], w_ref[...]), h_ref, g_ref, gn_ref, o_ref, hn_ref)


def _down_proj(a, w, h, g, g_next, emit_hn, tm):
    t, kk = a.shape
    d = w.shape[1]
    tok = pl.BlockSpec((tm, d), lambda i: (i, 0))
    vec = pl.BlockSpec((1, d), lambda i: (0, 0))
    out_specs = [tok]
    out_shape = [jax.ShapeDtypeStruct((t, d), F32)]
    if emit_hn:
        out_specs.append(tok)
        out_shape.append(jax.ShapeDtypeStruct((t, d), BF16))
    res = pl.pallas_call(
        functools.partial(_down_kernel, emit_hn=emit_hn),
        grid=(t // tm,),
        in_specs=[pl.BlockSpec((tm, kk), lambda i: (i, 0)),
                  pl.BlockSpec((kk, d), lambda i: (0, 0), **_RESIDENT),
                  tok, vec, vec],
        out_specs=out_specs,
        out_shape=out_shape,
        compiler_params=_params("parallel", vmem=VMEM_LIMIT_LARGE),
        name="down_proj",
    )(a, w, h, g, g_next)
    return res if emit_hn else (res[0], None)


def _sgu_down_kernel(u_ref, v_ref, lg_ref, lb_ref, ws_ref, bs_ref, w_ref,
                     h_ref, g_ref, gn_ref, o_ref, hn_ref, gate_ref, *, heads_per_group):
    tc, mt, half = u_ref.shape
    heads = ws_ref.shape[0]
    hd = half // heads
    cpb = SGU_BLOCK // tc
    cps = SEQ_CHUNK // tc

    row = lax.broadcasted_iota(jnp.int32, (SGU_BLOCK, SGU_BLOCK), 0)
    col = lax.broadcasted_iota(jnp.int32, (SGU_BLOCK, SGU_BLOCK), 1)
    causal = (col % cpb) // cps <= (row % cpb) // cps
    s1 = jnp.zeros((tc, mt, LANES), F32)
    s2 = jnp.zeros((tc, mt, LANES), F32)
    for c in range(half // LANES):
        vc = v_ref[:, :, pl.ds(c * LANES, LANES)].astype(F32)
        s1 += vc
        s2 += vc * vc
    mean = jnp.sum(s1, axis=-1, keepdims=True) * (1.0 / half)
    rstd = lax.rsqrt(jnp.sum(s2, axis=-1, keepdims=True) * (1.0 / half) - mean * mean + EPS)

    def block_of(x, n):
        return jnp.concatenate([x[o, n * cpb:(n + 1) * cpb] for o in range(tc)], axis=0)

    f = None
    for hh in range(heads):
        cols = pl.ds(hh * hd, hd)
        w = jnp.where(causal, ws_ref[hh], 0.0).astype(BF16)
        lg = lg_ref[:, cols]
        lb = lb_ref[:, cols]
        bias = bs_ref[hh]
        for n in range(mt // cpb):
            rows = pl.ds(n * cpb, cpb)
            v = jnp.concatenate([v_ref[o, rows, cols] for o in range(tc)], axis=0).astype(F32)
            vn = ((v - block_of(mean, n)) * block_of(rstd, n) * lg + lb).astype(BF16)
            s = _dot(w, vn) + bias
            u = jnp.concatenate([u_ref[o, rows, cols] for o in range(tc)], axis=0).astype(F32)
            gated = (u * s).astype(BF16)
            for o in range(tc):
                gate_ref[o, rows, cols] = gated[o * cpb:(o + 1) * cpb]
        if (hh + 1) % heads_per_group == 0:
            gcols = pl.ds((hh + 1 - heads_per_group) * hd, heads_per_group * hd)
            part = _dot(gate_ref[:, :, gcols].reshape(tc * mt, heads_per_group * hd),
                        w_ref[gcols, :])
            f = part if f is None else f + part
    _finish(f, h_ref, g_ref, gn_ref, o_ref, hn_ref)


def _sgu_down(z, ln_g, ln_b, w_s, b_s, w_out, h, g, g_next, mt, heads_per_group):
    tc, m, n = z.shape
    half = n // 2
    heads = w_s.shape[0]
    assert heads % heads_per_group == 0
    d = w_out.shape[1]
    tok = lambda c: pl.BlockSpec((tc, mt, c), lambda i: (0, i, 0))
    const = lambda *shape: pl.BlockSpec(shape, lambda i: (0,) * len(shape), **_RESIDENT)
    return pl.pallas_call(
        functools.partial(_sgu_down_kernel, heads_per_group=heads_per_group),
        grid=(m // mt,),
        in_specs=[pl.BlockSpec((tc, mt, half), lambda i: (0, i, 0)),
                  pl.BlockSpec((tc, mt, half), lambda i: (0, i, 1)),
                  const(1, half), const(1, half),
                  const(heads, SGU_BLOCK, SGU_BLOCK), const(heads, SGU_BLOCK, 1),
                  const(half, d),
                  tok(d), const(1, d), const(1, d)],
        out_specs=[tok(d), tok(d)],
        out_shape=[jax.ShapeDtypeStruct((tc, m, d), F32), jax.ShapeDtypeStruct((tc, m, d), BF16)],
        scratch_shapes=[pltpu.VMEM((tc, mt, half), BF16)],
        compiler_params=_params("parallel", vmem=VMEM_LIMIT_LARGE),
        name="sgu_down",
    )(z, z, ln_g, ln_b, w_s, b_s, w_out, h, g, g_next)


def _zoh(a_re, a_im, log_dt):
    lam_re = jnp.minimum(a_re, EIG_CLIP)
    lam_im = a_im
    dt = jnp.exp(log_dt)
    mag = jnp.exp(lam_re * dt)
    ab_re = mag * jnp.cos(lam_im * dt)
    ab_im = mag * jnp.sin(lam_im * dt)
    denom = lam_re * lam_re + lam_im * lam_im
    coef_re = ((ab_re - 1.0) * lam_re + ab_im * lam_im) / denom
    coef_im = (ab_im * lam_re - (ab_re - 1.0) * lam_im) / denom
    return ab_re, ab_im, coef_re, coef_im


def _cmul(x_re, x_im, y_re, y_im):
    return x_re * y_re - x_im * y_im, x_re * y_im + x_im * y_re


def _dot_3pass(a, b):
    a_hi = a.astype(BF16)
    a_lo = (a - a_hi.astype(F32)).astype(BF16)
    b_hi = b.astype(BF16)
    b_lo = (b - b_hi.astype(F32)).astype(BF16)
    return _dot(a_hi, b_hi) + _dot(a_hi, b_lo) + _dot(a_lo, b_hi)


def _s5_tables_kernel(are_ref, aim_ref, ldt_ref, bc1_ref, bc2_ref, cc_ref,
                      areb_ref, aimb_ref, ldtb_ref,
                      mi_ref, ms_ref, mc_ref, apr_ref, api_ref):
    tc, gl, gc, half = SSM_CHUNK, GROUPS_PER_TILE, S5_GROUP, LANES // 2
    lane = lax.broadcasted_iota(jnp.int32, (gc, LANES), 1)
    lower = lane < half
    sgn = jnp.where(lower, -1.0, 1.0)

    ab_re, ab_im, x_re, x_im = _zoh(are_ref[...], aim_ref[...], ldt_ref[...])
    bc1 = bc1_ref[...]
    bc2 = bc2_ref[...]
    ms_ref[...] = jnp.zeros(ms_ref.shape, ms_ref.dtype)
    w_lags = []
    for d in range(tc):
        w = x_re * bc1 + x_im * (sgn * bc2)
        wsw = x_re * bc2 - x_im * (sgn * bc1)
        w_lags.append(w)
        k = tc - 1 - d
        for g in range(gl):
            keep = lower if g % 2 == 0 else jnp.logical_not(lower)
            re_src, im_src = (w[g], wsw[g]) if g % 2 == 0 else (wsw[g], w[g])
            rows = pl.ds(k * LANES + g * gc, gc)
            ms_ref[rows, pl.ds((g // 2) * LANES, LANES)] = jnp.where(keep, re_src, 0.0).astype(BF16)
            ms_ref[rows, pl.ds((gl // 2 + g // 2) * LANES, LANES)] = (
                jnp.where(keep, im_src, 0.0).astype(BF16))
        x_re, x_im = _cmul(x_re, x_im, ab_re, ab_im)

    pw_re, pw_im = ab_re, ab_im
    for _ in range(tc - 1):
        pw_re, pw_im = _cmul(pw_re, pw_im, ab_re, ab_im)
    apr_ref[...] = pw_re
    api_ref[...] = pw_im

    row = lax.broadcasted_iota(jnp.int32, (LANES, LANES), 0)
    lane_sq = lax.broadcasted_iota(jnp.int32, (LANES, LANES), 1)
    c_sign = jnp.where(row < half, 1.0, -1.0)
    k_rows = []
    for g in range(gl):
        w_stack = jnp.concatenate([w_lags[d][g] for d in range(tc)], axis=0)
        k_g = _dot_3pass(w_stack, cc_ref[g] * c_sign)
        k_rows.append(jnp.where(lane_sq // gc == g, k_g, 0.0))
    zero = jnp.zeros((LANES, LANES), BF16)
    lag_blocks = [jnp.concatenate([k_rows[g][d * gc:(d + 1) * gc] for g in range(gl)],
                                  axis=0).astype(BF16) for d in range(tc)]
    for k in range(tc):
        for kp in range(tc):
            mi_ref[pl.ds(k * LANES, LANES), pl.ds(kp * LANES, LANES)] = (
                lag_blocks[kp - k] if kp >= k else zero)

    ab_re, ab_im, _, _ = _zoh(areb_ref[...], aimb_ref[...], ldtb_ref[...])
    c_re = cc_ref[:, :half, :]
    c_im = cc_ref[:, half:, :]
    grp = lax.broadcasted_iota(jnp.int32, (gl, half, LANES), 0)
    lane3 = lax.broadcasted_iota(jnp.int32, (gl, half, LANES), 2)
    diag = lane3 // gc == grp
    q_re, q_im = ab_re, ab_im
    for kp in range(tc):
        cm_re = jnp.where(diag, c_re * q_re - c_im * q_im, 0.0).astype(BF16)
        cm_im = jnp.where(diag, -(c_re * q_im + c_im * q_re), 0.0).astype(BF16)
        for g in range(gl):
            mc_ref[pl.ds(g * half, half), pl.ds(kp * LANES, LANES)] = cm_re[g]
            mc_ref[pl.ds((gl + g) * half, half), pl.ds(kp * LANES, LANES)] = cm_im[g]
        if kp + 1 < tc:
            q_re, q_im = _cmul(q_re, q_im, ab_re, ab_im)


def _s5_tables(a_re, a_im, log_dt, b_re, b_im, c_re, c_im):
    g, p = a_re.shape
    gc = b_re.shape[-1]
    gl = GROUPS_PER_TILE
    assert 2 * p == LANES and gc == S5_GROUP and g % gl == 0
    nj = g // gl
    kk = SSM_CHUNK * LANES
    ns = 2 * gl * p
    are_t = jnp.concatenate([a_re, a_re], axis=-1).reshape(g, 1, LANES)
    aim_t = jnp.concatenate([a_im, a_im], axis=-1).reshape(g, 1, LANES)
    ldt_t = jnp.broadcast_to(log_dt[:, None, None], (g, 1, LANES))
    bt_re = jnp.swapaxes(b_re, 1, 2)
    bt_im = jnp.swapaxes(b_im, 1, 2)
    bc1 = jnp.concatenate([bt_re, bt_im], axis=-1)
    bc2 = jnp.concatenate([bt_im, bt_re], axis=-1)
    ct = jnp.concatenate([jnp.swapaxes(c_re, 1, 2), jnp.swapaxes(c_im, 1, 2)], axis=1)
    cc = jnp.tile(ct, (1, 1, gl))
    lane_blk = lambda r, c: pl.BlockSpec((gl, r, c), lambda j: (j, 0, 0))
    mi, ms, mc, apr, api = pl.pallas_call(
        _s5_tables_kernel,
        grid=(nj,),
        in_specs=[lane_blk(1, LANES), lane_blk(1, LANES), lane_blk(1, LANES),
                  lane_blk(gc, LANES), lane_blk(gc, LANES), lane_blk(2 * p, LANES),
                  lane_blk(p, 1), lane_blk(p, 1), lane_blk(1, 1)],
        out_specs=[pl.BlockSpec((None, kk, kk), lambda j: (j, 0, 0)),
                   pl.BlockSpec((None, kk, ns), lambda j: (j, 0, 0)),
                   pl.BlockSpec((None, ns, kk), lambda j: (j, 0, 0)),
                   lane_blk(1, LANES), lane_blk(1, LANES)],
        out_shape=[jax.ShapeDtypeStruct((nj, kk, kk), BF16),
                   jax.ShapeDtypeStruct((nj, kk, ns), BF16),
                   jax.ShapeDtypeStruct((nj, ns, kk), BF16),
                   jax.ShapeDtypeStruct((g, 1, LANES), F32),
                   jax.ShapeDtypeStruct((g, 1, LANES), F32)],
        compiler_params=_params("parallel"),
        name="s5_tables",
    )(are_t, aim_t, ldt_t, bc1, bc2, cc, a_re[:, :, None], a_im[:, :, None],
      log_dt[:, None, None])
    pa_re = apr[:, 0, :p].reshape(1, g * p)
    pa_im = api[:, 0, :p].reshape(1, g * p)
    return mi, ms, mc, pa_re, pa_im


def _s5_in_kernel(x_ref, g_ref, w_ref, cg_ref, co_ref, u_ref, cgo_ref, coo_ref, wb_ref,
                  *, apply_norm):
    @pl.when((pl.program_id(0) == 0) & (pl.program_id(1) == 0))
    def _():
        wb_ref[...] = w_ref[...].astype(BF16)

    hn = _rms(x_ref[...], g_ref[...]).astype(BF16) if apply_norm else x_ref[...]
    u_ref[...] = _dot(hn, wb_ref[...]).astype(u_ref.dtype)
    cgo_ref[...] = cg_ref[...].astype(BF16)
    coo_ref[...] = co_ref[...].astype(BF16)


def _s5_in(x, g, w, w_glu, w_out, layer, apply_norm, tr):
    tc, m, d = x.shape
    nr = m // tr
    cr = _cast_rows(w_glu, tc * nr)
    tok = pl.BlockSpec((None, tr, d), lambda k, r: (k, r, 0))
    cast_in = pl.BlockSpec((None, cr, d), lambda k, r: (layer, k * nr + r, 0))
    cast_out = pl.BlockSpec((cr, d), lambda k, r: (k * nr + r, 0))
    return pl.pallas_call(
        functools.partial(_s5_in_kernel, apply_norm=apply_norm),
        grid=(tc, nr),
        in_specs=[tok,
                  pl.BlockSpec((1, d), lambda k, r: (0, 0)),
                  pl.BlockSpec((None, d, d), lambda k, r: (layer, 0, 0), **_RESIDENT),
                  cast_in, cast_in],
        out_specs=[tok, cast_out, cast_out],
        out_shape=[jax.ShapeDtypeStruct((tc, m, d), BF16),
                   jax.ShapeDtypeStruct((d, d), BF16), jax.ShapeDtypeStruct((d, d), BF16)],
        scratch_shapes=[pltpu.VMEM((d, d), BF16)],
        compiler_params=_params("arbitrary", "arbitrary"),
        name="s5_in",
    )(x, g, w, w_glu, w_out)


def _chunk_rows(u_ref):
    return jnp.concatenate([u_ref[k] for k in range(SSM_CHUNK)], axis=-1)


def _s5_state_kernel(u_ref, m_ref, zre_ref, zim_ref):
    z = _dot(_chunk_rows(u_ref), m_ref[...])
    half = z.shape[1] // 2
    zre_ref[...] = z[:, :half]
    zim_ref[...] = z[:, half:]


def _s5_state_in(u, m_state):
    tc, m, d = u.shape
    nj, kk, ns = m_state.shape
    half = ns // 2
    return pl.pallas_call(
        _s5_state_kernel,
        grid=(nj,),
        in_specs=[pl.BlockSpec((tc, m, LANES), lambda j: (0, 0, j)),
                  pl.BlockSpec((None, kk, ns), lambda j: (j, 0, 0))],
        out_specs=[pl.BlockSpec((m, half), lambda j: (0, j)),
                   pl.BlockSpec((m, half), lambda j: (0, j))],
        out_shape=[jax.ShapeDtypeStruct((m, nj * half), F32)] * 2,
        compiler_params=_params("parallel"),
        name="s5_state_in",
    )(u, m_state)


def _s5_scan_kernel(zre_ref, zim_ref, are_ref, aim_ref, sre_ref, sim_ref, st_re, st_im):
    rows = zre_ref.shape[0]

    @pl.when(pl.program_id(1) == 0)
    def _():
        st_re[...] = jnp.zeros_like(st_re)
        st_im[...] = jnp.zeros_like(st_im)

    a_re = are_ref[...]
    a_im = aim_ref[...]

    def step(i, carry):
        s_re, s_im = carry
        row = pl.ds(i, 1)
        sre_ref[row, :] = s_re
        sim_ref[row, :] = s_im
        n_re = a_re * s_re - a_im * s_im + zre_ref[row, :]
        n_im = a_re * s_im + a_im * s_re + zim_ref[row, :]
        return n_re, n_im

    s_re, s_im = lax.fori_loop(0, rows, step, (st_re[...], st_im[...]))
    st_re[...] = s_re
    st_im[...] = s_im


def _s5_scan(z_re, z_im, a_re, a_im, rows_per_seq, tr):
    m, n = z_re.shape
    blk = pl.BlockSpec((tr, n), lambda s, r: (s * (rows_per_seq // tr) + r, 0))
    coef = pl.BlockSpec((1, n), lambda s, r: (0, 0))
    return pl.pallas_call(
        _s5_scan_kernel,
        grid=(m // rows_per_seq, rows_per_seq // tr),
        in_specs=[blk, blk, coef, coef],
        out_specs=[blk, blk],
        out_shape=[jax.ShapeDtypeStruct((m, n), F32)] * 2,
        scratch_shapes=[pltpu.VMEM((1, n), F32), pltpu.VMEM((1, n), F32)],
        compiler_params=_params("arbitrary", "arbitrary"),
        name="s5_scan",
    )(z_re, z_im, a_re, a_im)


def _s5_mix_kernel(u_ref, sre_ref, sim_ref, mi_ref, mc_ref, y_ref):
    x = _chunk_rows(u_ref)
    s = jnp.concatenate([sre_ref[...], sim_ref[...]], axis=-1).astype(BF16)
    kk = mi_ref.shape[0]
    step = 2 * LANES
    for c0 in range(0, kk, step):
        cols = pl.ds(c0, step)
        y = _dot(x[:, :c0 + step], mi_ref[pl.ds(0, c0 + step), cols]) + _dot(s, mc_ref[:, cols])
        y_ref[:, cols] = y.astype(y_ref.dtype)


def _s5_mix(u, s_re, s_im, m_intra, m_carry):
    tc, m, d = u.shape
    nj, kk, _ = m_intra.shape
    ns = m_carry.shape[1]
    half = ns // 2
    return pl.pallas_call(
        _s5_mix_kernel,
        grid=(nj,),
        in_specs=[pl.BlockSpec((tc, m, LANES), lambda j: (0, 0, j)),
                  pl.BlockSpec((m, half), lambda j: (0, j)),
                  pl.BlockSpec((m, half), lambda j: (0, j)),
                  pl.BlockSpec((None, kk, kk), lambda j: (j, 0, 0)),
                  pl.BlockSpec((None, ns, kk), lambda j: (j, 0, 0))],
        out_specs=pl.BlockSpec((None, m, kk), lambda j: (j, 0, 0)),
        out_shape=jax.ShapeDtypeStruct((nj, m, kk), BF16),
        compiler_params=_params("parallel"),
        name="s5_mix",
    )(u, s_re, s_im, m_intra, m_carry)


def _s5_out_kernel(y_ref, u_ref, h_ref, dsk_ref, wg_ref, wo_ref, g_ref, gn_ref, o_ref, hn_ref,
                   *, row_piece):
    nj, tr, _ = y_ref.shape
    for r0 in range(0, tr, row_piece):
        rows = pl.ds(r0, row_piece)
        y = (jnp.concatenate([y_ref[j, rows, :] for j in range(nj)], axis=-1).astype(F32)
             + dsk_ref[...] * u_ref[rows, :].astype(F32))
        z = _gelu(y)
        gate = jax.nn.sigmoid(_dot(z.astype(BF16), wg_ref[...]))
        m = _dot((z * gate).astype(BF16), wo_ref[...])
        _finish(m, h_ref, g_ref, gn_ref, o_ref, hn_ref, rows)


def _s5_out(y, u, h, d_skip, w_glu, w_out, g, g_next, tr):
    nj, m, _ = y.shape
    tc, _, d = h.shape
    tok = pl.BlockSpec((None, tr, d), lambda k, r: (k, r, 0))
    vec = pl.BlockSpec((1, d), lambda k, r: (0, 0))
    mat = pl.BlockSpec((d, d), lambda k, r: (0, 0), **_RESIDENT)
    return pl.pallas_call(
        functools.partial(_s5_out_kernel, row_piece=_tile(tr, 256)),
        grid=(tc, m // tr),
        in_specs=[pl.BlockSpec((nj, tr, LANES), lambda k, r: (0, r, k)),
                  tok, tok, vec, mat, mat, vec, vec],
        out_specs=[tok, tok],
        out_shape=[jax.ShapeDtypeStruct(h.shape, F32), jax.ShapeDtypeStruct(h.shape, BF16)],
        compiler_params=_params("parallel", "parallel"),
        name="s5_out",
    )(y, u, h, d_skip, w_glu, w_out, g, g_next)


def _s5_layer(h, hn, layer, g_pre, g_post, g_next, w_in, a_re, a_im, log_dt, b_re, b_im,
              c_re, c_im, d_skip, w_glu, w_out, rows_per_seq):
    m_intra, m_state, m_carry, pa_re, pa_im = _s5_tables(a_re, a_im, log_dt, b_re, b_im,
                                                         c_re, c_im)
    r = rows_per_seq
    u, w_glu, w_out = _s5_in(h if hn is None else hn, g_pre, w_in, w_glu, w_out, layer,
                             hn is None, _tile(h.shape[1], 512 if hn is None else 1024))
    z_re, z_im = _s5_state_in(u, m_state)
    s_re, s_im = _s5_scan(z_re, z_im, pa_re, pa_im, r, _tile(r, 64))
    y = _s5_mix(u, s_re, s_im, m_intra, m_carry)
    return _s5_out(y, u, h, d_skip, w_glu, w_out, g_post, g_next, _tile(r, 512))


def _frames_by_offset(w, axes):
    cpb = SGU_BLOCK // SSM_CHUNK
    for ax in axes:
        shape = w.shape
        w = w.reshape(shape[:ax] + (cpb, SSM_CHUNK) + shape[ax + 1:])
        w = jnp.swapaxes(w, ax, ax + 1).reshape(shape)
    return w


def kernel(x, norm_g, s5_w_in, s5_a_re, s5_a_im, s5_log_dt, s5_b_re, s5_b_im, s5_c_re, s5_c_im,
           s5_d, s5_w_glu, s5_w_out, sgu_w_in, sgu_ln_g, sgu_ln_b, sgu_w_s, sgu_b_s, sgu_w_out,
           ffn_w_up, ffn_w_down):
    bsz, seq, d = x.shape
    t = bsz * seq
    tc = SSM_CHUNK
    r = seq // tc
    m = bsz * r
    depth = norm_g.shape[0]
    assert seq % SGU_BLOCK == 0 and d % LANES == 0 and SEQ_CHUNK % tc == 0
    gains = norm_g.reshape(depth, 4, 1, d)
    h = jnp.transpose(x.reshape(bsz, r, tc, d), (2, 0, 1, 3)).reshape(tc, m, d)
    hn = None
    for i in range(depth):
        g = gains[i]
        j = i // 2
        if i % 2 == 0:
            h, hn = _s5_layer(h, hn, j, g[0], g[1], g[2], s5_w_in, s5_a_re[j], s5_a_im[j],
                              s5_log_dt[j], s5_b_re[j], s5_b_im[j], s5_c_re[j], s5_c_im[j],
                              s5_d[j].reshape(1, d), s5_w_glu, s5_w_out, r)
        else:
            half = sgu_ln_g.shape[-1]
            heads = sgu_w_s.shape[1]
            z, w_out = _up_proj(hn.reshape(t, d), sgu_w_in, sgu_w_out, j, _gelu,
                                _tile(t, 2048), _tile(half, 1024))
            h, hn = _sgu_down(z.reshape(tc, m, 2 * half), sgu_ln_g[j].reshape(1, half),
                              sgu_ln_b[j].reshape(1, half), _frames_by_offset(sgu_w_s[j], (1, 2)),
                              _frames_by_offset(sgu_b_s[j], (1,)).reshape(heads, SGU_BLOCK, 1),
                              w_out, h, g[1], g[2], _tile(r, 32), _tile(heads, 2))
        a, w_down = _up_proj(hn.reshape(t, d), ffn_w_up, ffn_w_down, i, _relu2,
                             _tile(t, 2048), _tile(ffn_w_up.shape[-1], 1024))
        last = i == depth - 1
        h, hn = _down_proj(a, w_down, h.reshape(t, d), g[3], g[3] if last else gains[i + 1][0],
                           not last, _tile(t, 256))
        h = h.reshape(tc, m, d)
        hn = None if last else hn.reshape(tc, m, d)
    return jnp.transpose(h.reshape(tc, bsz, r, d), (1, 2, 0, 3)).reshape(bsz, seq, d)
```

```python
import functools
import math

import jax
import jax.numpy as jnp
from jax import lax
from jax.experimental import pallas as pl
from jax.experimental.pallas import tpu as pltpu

F32 = jnp.float32
BF16 = jnp.bfloat16

EPS = 1e-6
EIG_CLIP = -1e-4
SEQ_CHUNK = 64
SGU_BLOCK = 128
S5_GROUP = 16
SSM_CHUNK = 8
LANES = 128
GROUPS_PER_TILE = LANES // S5_GROUP
VMEM_LIMIT = 56 * 1024 * 1024
VMEM_LIMIT_LARGE = 60 * 1024 * 1024

_dot = functools.partial(jnp.dot, preferred_element_type=F32)
_RESIDENT = dict(pipeline_mode=pl.Buffered(1))


def _params(*sem, vmem=VMEM_LIMIT):
    return pltpu.CompilerParams(dimension_semantics=sem, vmem_limit_bytes=vmem)


def _rms(x, g):
    ms = jnp.mean(x * x, axis=-1, keepdims=True)
    return x * lax.rsqrt(ms + EPS) * g


def _relu2(y):
    a = jnp.maximum(y, 0.0)
    return a * a


_GELU_C1 = math.sqrt(2.0 / math.pi)
_GELU_C2 = _GELU_C1 * 0.044715


def _gelu(x):
    hx = 0.5 * x
    return hx + hx * jnp.tanh(x * (_GELU_C1 + _GELU_C2 * (x * x)))


def _tile(n, pref):
    t = min(n, pref)
    assert n % t == 0, (n, t)
    return t


def _up_kernel(x_ref, w_ref, cw_ref, o_ref, cwo_ref, wb_ref, *, act, row_piece):
    @pl.when(pl.program_id(1) == 0)
    def _():
        wb_ref[...] = w_ref[...].astype(BF16)

    tm = x_ref.shape[0]
    for r0 in range(0, tm, row_piece):
        rows = pl.ds(r0, row_piece)
        o_ref[rows, :] = act(_dot(x_ref[rows, :], wb_ref[...])).astype(o_ref.dtype)
    cwo_ref[...] = cw_ref[...].astype(BF16)


def _cast_rows(cast_w, steps):
    rows = cast_w.shape[1] // steps
    assert rows * steps == cast_w.shape[1] and rows % 16 == 0
    return rows


def _up_proj(hn, w, cast_w, layer, act, tm, tn):
    t, d = hn.shape
    n = w.shape[2]
    nm = t // tm
    cr = _cast_rows(cast_w, (n // tn) * nm)
    cc = cast_w.shape[2]
    return pl.pallas_call(
        functools.partial(_up_kernel, act=act, row_piece=_tile(tm, 1024)),
        grid=(n // tn, nm),
        in_specs=[pl.BlockSpec((tm, d), lambda j, i: (i, 0)),
                  pl.BlockSpec((None, d, tn), lambda j, i: (layer, 0, j)),
                  pl.BlockSpec((None, cr, cc), lambda j, i: (layer, j * nm + i, 0))],
        out_specs=[pl.BlockSpec((tm, tn), lambda j, i: (i, j)),
                   pl.BlockSpec((cr, cc), lambda j, i: (j * nm + i, 0))],
        out_shape=[jax.ShapeDtypeStruct((t, n), BF16),
                   jax.ShapeDtypeStruct(cast_w.shape[1:], BF16)],
        scratch_shapes=[pltpu.VMEM((d, tn), BF16)],
        compiler_params=_params("arbitrary", "arbitrary", vmem=VMEM_LIMIT_LARGE),
        name="up_proj",
    )(hn, w, cast_w)


def _finish(f, h_ref, g_ref, gn_ref, o_ref, hn_ref, rows=None):
    if rows is None:
        h = h_ref[...].reshape(f.shape) + _rms(f, g_ref[...])
        o_ref[...] = h.reshape(o_ref.shape)
        if hn_ref is not None:
            hn_ref[...] = _rms(h, gn_ref[...]).astype(BF16).reshape(hn_ref.shape)
    else:
        h = h_ref[rows, :] + _rms(f, g_ref[...])
        o_ref[rows, :] = h
        if hn_ref is not None:
            hn_ref[rows, :] = _rms(h, gn_ref[...]).astype(BF16)


def _down_kernel(a_ref, w_ref, h_ref, g_ref, gn_ref, o_ref, *rest, emit_hn):
    hn_ref = rest[0] if emit_hn else None
    _finish(_dot(a_ref[...], w_ref[...]), h_ref, g_ref, gn_ref, o_ref, hn_ref)


def _down_proj(a, w, h, g, g_next, emit_hn, tm):
    t, kk = a.shape
    d = w.shape[1]
    tok = pl.BlockSpec((tm, d), lambda i: (i, 0))
    vec = pl.BlockSpec((1, d), lambda i: (0, 0))
    out_specs = [tok]
    out_shape = [jax.ShapeDtypeStruct((t, d), F32)]
    if emit_hn:
        out_specs.append(tok)
        out_shape.append(jax.ShapeDtypeStruct((t, d), BF16))
    res = pl.pallas_call(
        functools.partial(_down_kernel, emit_hn=emit_hn),
        grid=(t // tm,),
        in_specs=[pl.BlockSpec((tm, kk), lambda i: (i, 0)),
                  pl.BlockSpec((kk, d), lambda i: (0, 0), **_RESIDENT),
                  tok, vec, vec],
        out_specs=out_specs,
        out_shape=out_shape,
        compiler_params=_params("parallel", vmem=VMEM_LIMIT_LARGE),
        name="down_proj",
    )(a, w, h, g, g_next)
    return res if emit_hn else (res[0], None)


def _sgu_down_kernel(u_ref, v_ref, lg_ref, lb_ref, ws_ref, bs_ref, w_ref,
                     h_ref, g_ref, gn_ref, o_ref, hn_ref, gate_ref, *, heads_per_group):
    tc, mt, half = u_ref.shape
    heads = ws_ref.shape[0]
    hd = half // heads
    cpb = SGU_BLOCK // tc
    cps = SEQ_CHUNK // tc

    row = lax.broadcasted_iota(jnp.int32, (SGU_BLOCK, SGU_BLOCK), 0)
    col = lax.broadcasted_iota(jnp.int32, (SGU_BLOCK, SGU_BLOCK), 1)
    causal = (col % cpb) // cps <= (row % cpb) // cps
    s1 = jnp.zeros((tc, mt, LANES), F32)
    s2 = jnp.zeros((tc, mt, LANES), F32)
    for c in range(half // LANES):
        vc = v_ref[:, :, pl.ds(c * LANES, LANES)].astype(F32)
        s1 += vc
        s2 += vc * vc
    mean = jnp.sum(s1, axis=-1, keepdims=True) * (1.0 / half)
    rstd = lax.rsqrt(jnp.sum(s2, axis=-1, keepdims=True) * (1.0 / half) - mean * mean + EPS)

    def block_of(x, n):
        return jnp.concatenate([x[o, n * cpb:(n + 1) * cpb] for o in range(tc)], axis=0)

    f = None
    for hh in range(heads):
        cols = pl.ds(hh * hd, hd)
        w = jnp.where(causal, ws_ref[hh], 0.0).astype(BF16)
        lg = lg_ref[:, cols]
        lb = lb_ref[:, cols]
        bias = bs_ref[hh]
        for n in range(mt // cpb):
            rows = pl.ds(n * cpb, cpb)
            v = jnp.concatenate([v_ref[o, rows, cols] for o in range(tc)], axis=0).astype(F32)
            vn = ((v - block_of(mean, n)) * block_of(rstd, n) * lg + lb).astype(BF16)
            s = _dot(w, vn) + bias
            u = jnp.concatenate([u_ref[o, rows, cols] for o in range(tc)], axis=0).astype(F32)
            gated = (u * s).astype(BF16)
            for o in range(tc):
                gate_ref[o, rows, cols] = gated[o * cpb:(o + 1) * cpb]
        if (hh + 1) % heads_per_group == 0:
            gcols = pl.ds((hh + 1 - heads_per_group) * hd, heads_per_group * hd)
            part = _dot(gate_ref[:, :, gcols].reshape(tc * mt, heads_per_group * hd),
                        w_ref[gcols, :])
            f = part if f is None else f + part
    _finish(f, h_ref, g_ref, gn_ref, o_ref, hn_ref)


def _sgu_down(z, ln_g, ln_b, w_s, b_s, w_out, h, g, g_next, mt, heads_per_group):
    tc, m, n = z.shape
    half = n // 2
    heads = w_s.shape[0]
    assert heads % heads_per_group == 0
    d = w_out.shape[1]
    tok = lambda c: pl.BlockSpec((tc, mt, c), lambda i: (0, i, 0))
    const = lambda *shape: pl.BlockSpec(shape, lambda i: (0,) * len(shape), **_RESIDENT)
    return pl.pallas_call(
        functools.partial(_sgu_down_kernel, heads_per_group=heads_per_group),
        grid=(m // mt,),
        in_specs=[pl.BlockSpec((tc, mt, half), lambda i: (0, i, 0)),
                  pl.BlockSpec((tc, mt, half), lambda i: (0, i, 1)),
                  const(1, half), const(1, half),
                  const(heads, SGU_BLOCK, SGU_BLOCK), const(heads, SGU_BLOCK, 1),
                  const(half, d),
                  tok(d), const(1, d), const(1, d)],
        out_specs=[tok(d), tok(d)],
        out_shape=[jax.ShapeDtypeStruct((tc, m, d), F32), jax.ShapeDtypeStruct((tc, m, d), BF16)],
        scratch_shapes=[pltpu.VMEM((tc, mt, half), BF16)],
        compiler_params=_params("parallel", vmem=VMEM_LIMIT_LARGE),
        name="sgu_down",
    )(z, z, ln_g, ln_b, w_s, b_s, w_out, h, g, g_next)


def _zoh(a_re, a_im, log_dt):
    lam_re = jnp.minimum(a_re, EIG_CLIP)
    lam_im = a_im
    dt = jnp.exp(log_dt)
    mag = jnp.exp(lam_re * dt)
    ab_re = mag * jnp.cos(lam_im * dt)
    ab_im = mag * jnp.sin(lam_im * dt)
    denom = lam_re * lam_re + lam_im * lam_im
    coef_re = ((ab_re - 1.0) * lam_re + ab_im * lam_im) / denom
    coef_im = (ab_im * lam_re - (ab_re - 1.0) * lam_im) / denom
    return ab_re, ab_im, coef_re, coef_im


def _cmul(x_re, x_im, y_re, y_im):
    return x_re * y_re - x_im * y_im, x_re * y_im + x_im * y_re


def _dot_3pass(a, b):
    a_hi = a.astype(BF16)
    a_lo = (a - a_hi.astype(F32)).astype(BF16)
    b_hi = b.astype(BF16)
    b_lo = (b - b_hi.astype(F32)).astype(BF16)
    return _dot(a_hi, b_hi) + _dot(a_hi, b_lo) + _dot(a_lo, b_hi)


def _s5_tables_kernel(are_ref, aim_ref, ldt_ref, bc1_ref, bc2_ref, cc_ref,
                      areb_ref, aimb_ref, ldtb_ref,
                      mi_ref, ms_ref, mc_ref, apr_ref, api_ref):
    tc, gl, gc, half = SSM_CHUNK, GROUPS_PER_TILE, S5_GROUP, LANES // 2
    lane = lax.broadcasted_iota(jnp.int32, (gc, LANES), 1)
    lower = lane < half
    sgn = jnp.where(lower, -1.0, 1.0)

    ab_re, ab_im, x_re, x_im = _zoh(are_ref[...], aim_ref[...], ldt_ref[...])
    bc1 = bc1_ref[...]
    bc2 = bc2_ref[...]
    ms_ref[...] = jnp.zeros(ms_ref.shape, ms_ref.dtype)
    w_lags = []
    for d in range(tc):
        w = x_re * bc1 + x_im * (sgn * bc2)
        wsw = x_re * bc2 - x_im * (sgn * bc1)
        w_lags.append(w)
        k = tc - 1 - d
        for g in range(gl):
            keep = lower if g % 2 == 0 else jnp.logical_not(lower)
            re_src, im_src = (w[g], wsw[g]) if g % 2 == 0 else (wsw[g], w[g])
            rows = pl.ds(k * LANES + g * gc, gc)
            ms_ref[rows, pl.ds((g // 2) * LANES, LANES)] = jnp.where(keep, re_src, 0.0).astype(BF16)
            ms_ref[rows, pl.ds((gl // 2 + g // 2) * LANES, LANES)] = (
                jnp.where(keep, im_src, 0.0).astype(BF16))
        x_re, x_im = _cmul(x_re, x_im, ab_re, ab_im)

    pw_re, pw_im = ab_re, ab_im
    for _ in range(tc - 1):
        pw_re, pw_im = _cmul(pw_re, pw_im, ab_re, ab_im)
    apr_ref[...] = pw_re
    api_ref[...] = pw_im

    row = lax.broadcasted_iota(jnp.int32, (LANES, LANES), 0)
    lane_sq = lax.broadcasted_iota(jnp.int32, (LANES, LANES), 1)
    c_sign = jnp.where(row < half, 1.0, -1.0)
    k_rows = []
    for g in range(gl):
        w_stack = jnp.concatenate([w_lags[d][g] for d in range(tc)], axis=0)
        k_g = _dot_3pass(w_stack, cc_ref[g] * c_sign)
        k_rows.append(jnp.where(lane_sq // gc == g, k_g, 0.0))
    zero = jnp.zeros((LANES, LANES), BF16)
    lag_blocks = [jnp.concatenate([k_rows[g][d * gc:(d + 1) * gc] for g in range(gl)],
                                  axis=0).astype(BF16) for d in range(tc)]
    for k in range(tc):
        for kp in range(tc):
            mi_ref[pl.ds(k * LANES, LANES), pl.ds(kp * LANES, LANES)] = (
                lag_blocks[kp - k] if kp >= k else zero)

    ab_re, ab_im, _, _ = _zoh(areb_ref[...], aimb_ref[...], ldtb_ref[...])
    c_re = cc_ref[:, :half, :]
    c_im = cc_ref[:, half:, :]
    grp = lax.broadcasted_iota(jnp.int32, (gl, half, LANES), 0)
    lane3 = lax.broadcasted_iota(jnp.int32, (gl, half, LANES), 2)
    diag = lane3 // gc == grp
    q_re, q_im = ab_re, ab_im
    for kp in range(tc):
        cm_re = jnp.where(diag, c_re * q_re - c_im * q_im, 0.0).astype(BF16)
        cm_im = jnp.where(diag, -(c_re * q_im + c_im * q_re), 0.0).astype(BF16)
        for g in range(gl):
            mc_ref[pl.ds(g * half, half), pl.ds(kp * LANES, LANES)] = cm_re[g]
            mc_ref[pl.ds((gl + g) * half, half), pl.ds(kp * LANES, LANES)] = cm_im[g]
        if kp + 1 < tc:
            q_re, q_im = _cmul(q_re, q_im, ab_re, ab_im)


def _s5_tables(a_re, a_im, log_dt, b_re, b_im, c_re, c_im):
    g, p = a_re.shape
    gc = b_re.shape[-1]
    gl = GROUPS_PER_TILE
    assert 2 * p == LANES and gc == S5_GROUP and g % gl == 0
    nj = g // gl
    kk = SSM_CHUNK * LANES
    ns = 2 * gl * p
    are_t = jnp.concatenate([a_re, a_re], axis=-1).reshape(g, 1, LANES)
    aim_t = jnp.concatenate([a_im, a_im], axis=-1).reshape(g, 1, LANES)
    ldt_t = jnp.broadcast_to(log_dt[:, None, None], (g, 1, LANES))
    bt_re = jnp.swapaxes(b_re, 1, 2)
    bt_im = jnp.swapaxes(b_im, 1, 2)
    bc1 = jnp.concatenate([bt_re, bt_im], axis=-1)
    bc2 = jnp.concatenate([bt_im, bt_re], axis=-1)
    ct = jnp.concatenate([jnp.swapaxes(c_re, 1, 2), jnp.swapaxes(c_im, 1, 2)], axis=1)
    cc = jnp.tile(ct, (1, 1, gl))
    lane_blk = lambda r, c: pl.BlockSpec((gl, r, c), lambda j: (j, 0, 0))
    mi, ms, mc, apr, api = pl.pallas_call(
        _s5_tables_kernel,
        grid=(nj,),
        in_specs=[lane_blk(1, LANES), lane_blk(1, LANES), lane_blk(1, LANES),
                  lane_blk(gc, LANES), lane_blk(gc, LANES), lane_blk(2 * p, LANES),
                  lane_blk(p, 1), lane_blk(p, 1), lane_blk(1, 1)],
        out_specs=[pl.BlockSpec((None, kk, kk), lambda j: (j, 0, 0)),
                   pl.BlockSpec((None, kk, ns), lambda j: (j, 0, 0)),
                   pl.BlockSpec((None, ns, kk), lambda j: (j, 0, 0)),
                   lane_blk(1, LANES), lane_blk(1, LANES)],
        out_shape=[jax.ShapeDtypeStruct((nj, kk, kk), BF16),
                   jax.ShapeDtypeStruct((nj, kk, ns), BF16),
                   jax.ShapeDtypeStruct((nj, ns, kk), BF16),
                   jax.ShapeDtypeStruct((g, 1, LANES), F32),
                   jax.ShapeDtypeStruct((g, 1, LANES), F32)],
        compiler_params=_params("parallel"),
        name="s5_tables",
    )(are_t, aim_t, ldt_t, bc1, bc2, cc, a_re[:, :, None], a_im[:, :, None],
      log_dt[:, None, None])
    pa_re = apr[:, 0, :p].reshape(1, g * p)
    pa_im = api[:, 0, :p].reshape(1, g * p)
    return mi, ms, mc, pa_re, pa_im


def _s5_in_kernel(x_ref, g_ref, w_ref, cg_ref, co_ref, u_ref, *rest, from_frames):
    k, r = pl.program_id(0), pl.program_id(1)
    nr = pl.num_programs(1)
    step = k * nr + r

    if from_frames:
        hp_ref, cgo_ref, coo_ref, wb_ref, xbuf, sem = rest

        def plane_copy(s, slot):
            return pltpu.make_async_copy(x_ref.at[s % nr, :, s // nr, :], xbuf.at[slot],
                                         sem.at[slot])

        @pl.when(step == 0)
        def _():
            plane_copy(step, 0).start()

        @pl.when(step + 1 < pl.num_programs(0) * nr)
        def _():
            plane_copy(step + 1, (step + 1) % 2).start()

        plane_copy(step, step % 2).wait()
    else:
        cgo_ref, coo_ref, wb_ref = rest

    @pl.when(step == 0)
    def _():
        wb_ref[...] = w_ref[...].astype(BF16)

    if from_frames:
        x = xbuf[step % 2]
        hp_ref[...] = x
        hn = _rms(x, g_ref[...]).astype(BF16)
    else:
        hn = x_ref[...]
    u_ref[...] = _dot(hn, wb_ref[...]).astype(u_ref.dtype)
    cgo_ref[...] = cg_ref[...].astype(BF16)
    coo_ref[...] = co_ref[...].astype(BF16)


def _s5_in(x, g, w, w_glu, w_out, layer, from_frames, tr):
    if from_frames:
        nr, _, tc, d = x.shape
        m = nr * tr
    else:
        tc, m, d = x.shape
        nr = m // tr
    cr = _cast_rows(w_glu, tc * nr)
    tok = pl.BlockSpec((None, tr, d), lambda k, r: (k, r, 0))
    cast_in = pl.BlockSpec((None, cr, d), lambda k, r: (layer, k * nr + r, 0))
    cast_out = pl.BlockSpec((cr, d), lambda k, r: (k * nr + r, 0))
    casts = [jax.ShapeDtypeStruct((d, d), BF16)] * 2
    out_specs = [tok, cast_out, cast_out]
    out_shape = [jax.ShapeDtypeStruct((tc, m, d), BF16)] + casts
    scratch = [pltpu.VMEM((d, d), BF16)]
    if from_frames:
        out_specs.insert(1, tok)
        out_shape.insert(1, jax.ShapeDtypeStruct((tc, m, d), F32))
        scratch += [pltpu.VMEM((2, tr, d), F32), pltpu.SemaphoreType.DMA((2,))]
    res = pl.pallas_call(
        functools.partial(_s5_in_kernel, from_frames=from_frames),
        grid=(tc, nr),
        in_specs=[pl.BlockSpec(memory_space=pl.ANY) if from_frames else tok,
                  pl.BlockSpec((1, d), lambda k, r: (0, 0)),
                  pl.BlockSpec((None, d, d), lambda k, r: (layer, 0, 0), **_RESIDENT),
                  cast_in, cast_in],
        out_specs=out_specs,
        out_shape=out_shape,
        scratch_shapes=scratch,
        compiler_params=_params("arbitrary", "arbitrary"),
        name="s5_in",
    )(x, g, w, w_glu, w_out)
    return res if from_frames else (res[0], None, res[1], res[2])


def _chunk_rows(u_ref):
    return jnp.concatenate([u_ref[k] for k in range(SSM_CHUNK)], axis=-1)


def _s5_state_kernel(u_ref, m_ref, zre_ref, zim_ref):
    z = _dot(_chunk_rows(u_ref), m_ref[...])
    half = z.shape[1] // 2
    zre_ref[...] = z[:, :half]
    zim_ref[...] = z[:, half:]


def _s5_state_in(u, m_state):
    tc, m, d = u.shape
    nj, kk, ns = m_state.shape
    half = ns // 2
    return pl.pallas_call(
        _s5_state_kernel,
        grid=(nj,),
        in_specs=[pl.BlockSpec((tc, m, LANES), lambda j: (0, 0, j)),
                  pl.BlockSpec((None, kk, ns), lambda j: (j, 0, 0))],
        out_specs=[pl.BlockSpec((m, half), lambda j: (0, j)),
                   pl.BlockSpec((m, half), lambda j: (0, j))],
        out_shape=[jax.ShapeDtypeStruct((m, nj * half), F32)] * 2,
        compiler_params=_params("parallel"),
        name="s5_state_in",
    )(u, m_state)


def _s5_scan_kernel(zre_ref, zim_ref, are_ref, aim_ref, sre_ref, sim_ref, st_re, st_im):
    rows = zre_ref.shape[0]

    @pl.when(pl.program_id(1) == 0)
    def _():
        st_re[...] = jnp.zeros_like(st_re)
        st_im[...] = jnp.zeros_like(st_im)

    a_re = are_ref[...]
    a_im = aim_ref[...]

    def step(i, carry):
        s_re, s_im = carry
        row = pl.ds(i, 1)
        sre_ref[row, :] = s_re
        sim_ref[row, :] = s_im
        n_re = a_re * s_re - a_im * s_im + zre_ref[row, :]
        n_im = a_re * s_im + a_im * s_re + zim_ref[row, :]
        return n_re, n_im

    s_re, s_im = lax.fori_loop(0, rows, step, (st_re[...], st_im[...]))
    st_re[...] = s_re
    st_im[...] = s_im


def _s5_scan(z_re, z_im, a_re, a_im, rows_per_seq, tr):
    m, n = z_re.shape
    blk = pl.BlockSpec((tr, n), lambda s, r: (s * (rows_per_seq // tr) + r, 0))
    coef = pl.BlockSpec((1, n), lambda s, r: (0, 0))
    return pl.pallas_call(
        _s5_scan_kernel,
        grid=(m // rows_per_seq, rows_per_seq // tr),
        in_specs=[blk, blk, coef, coef],
        out_specs=[blk, blk],
        out_shape=[jax.ShapeDtypeStruct((m, n), F32)] * 2,
        scratch_shapes=[pltpu.VMEM((1, n), F32), pltpu.VMEM((1, n), F32)],
        compiler_params=_params("arbitrary", "arbitrary"),
        name="s5_scan",
    )(z_re, z_im, a_re, a_im)


def _s5_mix_kernel(u_ref, sre_ref, sim_ref, mi_ref, mc_ref, y_ref):
    x = _chunk_rows(u_ref)
    s = jnp.concatenate([sre_ref[...], sim_ref[...]], axis=-1).astype(BF16)
    kk = mi_ref.shape[0]
    step = 2 * LANES
    for c0 in range(0, kk, step):
        cols = pl.ds(c0, step)
        y = _dot(x[:, :c0 + step], mi_ref[pl.ds(0, c0 + step), cols]) + _dot(s, mc_ref[:, cols])
        y_ref[:, cols] = y.astype(y_ref.dtype)


def _s5_mix(u, s_re, s_im, m_intra, m_carry):
    tc, m, d = u.shape
    nj, kk, _ = m_intra.shape
    ns = m_carry.shape[1]
    half = ns // 2
    return pl.pallas_call(
        _s5_mix_kernel,
        grid=(nj,),
        in_specs=[pl.BlockSpec((tc, m, LANES), lambda j: (0, 0, j)),
                  pl.BlockSpec((m, half), lambda j: (0, j)),
                  pl.BlockSpec((m, half), lambda j: (0, j)),
                  pl.BlockSpec((None, kk, kk), lambda j: (j, 0, 0)),
                  pl.BlockSpec((None, ns, kk), lambda j: (j, 0, 0))],
        out_specs=pl.BlockSpec((None, m, kk), lambda j: (j, 0, 0)),
        out_shape=jax.ShapeDtypeStruct((nj, m, kk), BF16),
        compiler_params=_params("parallel"),
        name="s5_mix",
    )(u, s_re, s_im, m_intra, m_carry)


def _s5_out_kernel(y_ref, u_ref, h_ref, dsk_ref, wg_ref, wo_ref, g_ref, gn_ref, o_ref, hn_ref,
                   *, row_piece):
    nj, tr, _ = y_ref.shape
    for r0 in range(0, tr, row_piece):
        rows = pl.ds(r0, row_piece)
        y = (jnp.concatenate([y_ref[j, rows, :] for j in range(nj)], axis=-1).astype(F32)
             + dsk_ref[...] * u_ref[rows, :].astype(F32))
        z = _gelu(y)
        gate = jax.nn.sigmoid(_dot(z.astype(BF16), wg_ref[...]))
        m = _dot((z * gate).astype(BF16), wo_ref[...])
        _finish(m, h_ref, g_ref, gn_ref, o_ref, hn_ref, rows)


def _s5_out(y, u, h, d_skip, w_glu, w_out, g, g_next, tr):
    nj, m, _ = y.shape
    tc, _, d = h.shape
    tok = pl.BlockSpec((None, tr, d), lambda k, r: (k, r, 0))
    vec = pl.BlockSpec((1, d), lambda k, r: (0, 0))
    mat = pl.BlockSpec((d, d), lambda k, r: (0, 0), **_RESIDENT)
    return pl.pallas_call(
        functools.partial(_s5_out_kernel, row_piece=_tile(tr, 256)),
        grid=(tc, m // tr),
        in_specs=[pl.BlockSpec((nj, tr, LANES), lambda k, r: (0, r, k)),
                  tok, tok, vec, mat, mat, vec, vec],
        out_specs=[tok, tok],
        out_shape=[jax.ShapeDtypeStruct(h.shape, F32), jax.ShapeDtypeStruct(h.shape, BF16)],
        compiler_params=_params("parallel", "parallel"),
        name="s5_out",
    )(y, u, h, d_skip, w_glu, w_out, g, g_next)


def _s5_layer(h, hn, layer, g_pre, g_post, g_next, w_in, a_re, a_im, log_dt, b_re, b_im,
              c_re, c_im, d_skip, w_glu, w_out, rows_per_seq):
    m_intra, m_state, m_carry, pa_re, pa_im = _s5_tables(a_re, a_im, log_dt, b_re, b_im,
                                                         c_re, c_im)
    r = rows_per_seq
    if hn is None:
        u, h, w_glu, w_out = _s5_in(h, g_pre, w_in, w_glu, w_out, layer, True, h.shape[1])
    else:
        u, _, w_glu, w_out = _s5_in(hn, g_pre, w_in, w_glu, w_out, layer, False,
                                    _tile(h.shape[1], 1024))
    z_re, z_im = _s5_state_in(u, m_state)
    s_re, s_im = _s5_scan(z_re, z_im, pa_re, pa_im, r, _tile(r, 64))
    y = _s5_mix(u, s_re, s_im, m_intra, m_carry)
    return _s5_out(y, u, h, d_skip, w_glu, w_out, g_post, g_next, _tile(r, 512))


def _frames_by_offset(w, axes):
    cpb = SGU_BLOCK // SSM_CHUNK
    for ax in axes:
        shape = w.shape
        w = w.reshape(shape[:ax] + (cpb, SSM_CHUNK) + shape[ax + 1:])
        w = jnp.swapaxes(w, ax, ax + 1).reshape(shape)
    return w


def kernel(x, norm_g, s5_w_in, s5_a_re, s5_a_im, s5_log_dt, s5_b_re, s5_b_im, s5_c_re, s5_c_im,
           s5_d, s5_w_glu, s5_w_out, sgu_w_in, sgu_ln_g, sgu_ln_b, sgu_w_s, sgu_b_s, sgu_w_out,
           ffn_w_up, ffn_w_down):
    bsz, seq, d = x.shape
    t = bsz * seq
    tc = SSM_CHUNK
    r = seq // tc
    m = bsz * r
    depth = norm_g.shape[0]
    assert seq % SGU_BLOCK == 0 and d % LANES == 0 and SEQ_CHUNK % tc == 0
    gains = norm_g.reshape(depth, 4, 1, d)
    tr0 = _tile(r, 512)
    h = x.reshape(m // tr0, tr0, tc, d)
    hn = None
    for i in range(depth):
        g = gains[i]
        j = i // 2
        if i % 2 == 0:
            h, hn = _s5_layer(h, hn, j, g[0], g[1], g[2], s5_w_in, s5_a_re[j], s5_a_im[j],
                              s5_log_dt[j], s5_b_re[j], s5_b_im[j], s5_c_re[j], s5_c_im[j],
                              s5_d[j].reshape(1, d), s5_w_glu, s5_w_out, r)
        else:
            half = sgu_ln_g.shape[-1]
            heads = sgu_w_s.shape[1]
            z, w_out = _up_proj(hn.reshape(t, d), sgu_w_in, sgu_w_out, j, _gelu,
                                _tile(t, 2048), _tile(half, 1024))
            h, hn = _sgu_down(z.reshape(tc, m, 2 * half), sgu_ln_g[j].reshape(1, half),
                              sgu_ln_b[j].reshape(1, half), _frames_by_offset(sgu_w_s[j], (1, 2)),
                              _frames_by_offset(sgu_b_s[j], (1,)).reshape(heads, SGU_BLOCK, 1),
                              w_out, h, g[1], g[2], _tile(r, 32), _tile(heads, 2))
        a, w_down = _up_proj(hn.reshape(t, d), ffn_w_up, ffn_w_down, i, _relu2,
                             _tile(t, 2048), _tile(ffn_w_up.shape[-1], 1024))
        last = i == depth - 1
        h, hn = _down_proj(a, w_down, h.reshape(t, d), g[3], g[3] if last else gains[i + 1][0],
                           not last, _tile(t, 256))
        h = h.reshape(tc, m, d)
        hn = None if last else hn.reshape(tc, m, d)
    return jnp.transpose(h.reshape(tc, bsz, r, d), (1, 2, 0, 3)).reshape(bsz, seq, d)
```

```python
import functools
import math

import jax
import jax.numpy as jnp
from jax import lax
from jax.experimental import pallas as pl
from jax.experimental.pallas import tpu as pltpu

F32 = jnp.float32
BF16 = jnp.bfloat16

EPS = 1e-6
EIG_CLIP = -1e-4
SEQ_CHUNK = 64
SGU_BLOCK = 128
S5_GROUP = 16
SSM_CHUNK = 8
LANES = 128
GROUPS_PER_TILE = LANES // S5_GROUP
VMEM_LIMIT = 56 * 1024 * 1024
VMEM_LIMIT_LARGE = 60 * 1024 * 1024

_dot = functools.partial(jnp.dot, preferred_element_type=F32)
_RESIDENT = dict(pipeline_mode=pl.Buffered(1))


def _params(*sem, vmem=VMEM_LIMIT):
    return pltpu.CompilerParams(dimension_semantics=sem, vmem_limit_bytes=vmem)


def _rms(x, g):
    ms = jnp.mean(x * x, axis=-1, keepdims=True)
    return x * lax.rsqrt(ms + EPS) * g


def _relu2(y):
    a = jnp.maximum(y, 0.0)
    return a * a


_GELU_C1 = math.sqrt(2.0 / math.pi)
_GELU_C2 = _GELU_C1 * 0.044715


def _gelu(x):
    hx = 0.5 * x
    return hx + hx * jnp.tanh(x * (_GELU_C1 + _GELU_C2 * (x * x)))


def _tile(n, pref):
    t = min(n, pref)
    assert n % t == 0, (n, t)
    return t


def _up_kernel(x_ref, w_ref, cw_ref, o_ref, cwo_ref, wb_ref, *, act, row_piece):
    @pl.when(pl.program_id(1) == 0)
    def _():
        wb_ref[...] = w_ref[...].astype(BF16)

    tm = x_ref.shape[0]
    for r0 in range(0, tm, row_piece):
        rows = pl.ds(r0, row_piece)
        o_ref[rows, :] = act(_dot(x_ref[rows, :], wb_ref[...])).astype(o_ref.dtype)
    cwo_ref[...] = cw_ref[...].astype(BF16)


def _cast_rows(cast_w, steps):
    rows = cast_w.shape[1] // steps
    assert rows * steps == cast_w.shape[1] and rows % 16 == 0
    return rows


def _up_proj(hn, w, cast_w, layer, act, tm, tn):
    t, d = hn.shape
    n = w.shape[2]
    nm = t // tm
    cr = _cast_rows(cast_w, (n // tn) * nm)
    cc = cast_w.shape[2]
    return pl.pallas_call(
        functools.partial(_up_kernel, act=act, row_piece=_tile(tm, 1024)),
        grid=(n // tn, nm),
        in_specs=[pl.BlockSpec((tm, d), lambda j, i: (i, 0)),
                  pl.BlockSpec((None, d, tn), lambda j, i: (layer, 0, j)),
                  pl.BlockSpec((None, cr, cc), lambda j, i: (layer, j * nm + i, 0))],
        out_specs=[pl.BlockSpec((tm, tn), lambda j, i: (i, j)),
                   pl.BlockSpec((cr, cc), lambda j, i: (j * nm + i, 0))],
        out_shape=[jax.ShapeDtypeStruct((t, n), BF16),
                   jax.ShapeDtypeStruct(cast_w.shape[1:], BF16)],
        scratch_shapes=[pltpu.VMEM((d, tn), BF16)],
        compiler_params=_params("arbitrary", "arbitrary", vmem=VMEM_LIMIT_LARGE),
        name="up_proj",
    )(hn, w, cast_w)


def _finish(f, h_ref, g_ref, gn_ref, o_ref, hn_ref, rows=None):
    if rows is None:
        h = h_ref[...].reshape(f.shape) + _rms(f, g_ref[...])
        o_ref[...] = h.reshape(o_ref.shape)
        if hn_ref is not None:
            hn_ref[...] = _rms(h, gn_ref[...]).astype(BF16).reshape(hn_ref.shape)
    else:
        h = h_ref[rows, :] + _rms(f, g_ref[...])
        o_ref[rows, :] = h
        if hn_ref is not None:
            hn_ref[rows, :] = _rms(h, gn_ref[...]).astype(BF16)


def _down_kernel(a_ref, w_ref, h_ref, g_ref, gn_ref, o_ref, *rest, to_frames):
    f = _dot(a_ref[...], w_ref[...])
    if not to_frames:
        _finish(f, h_ref, g_ref, gn_ref, o_ref, rest[0])
        return
    obuf, sem = rest
    i = pl.program_id(0)
    n = pl.num_programs(0)
    row_tiles = o_ref.shape[0]

    def plane_copy(s, slot):
        return pltpu.make_async_copy(obuf.at[slot], o_ref.at[s % row_tiles, :, s // row_tiles, :],
                                     sem.at[slot])

    slot = i % 2

    @pl.when(i >= 2)
    def _():
        plane_copy(i - 2, slot).wait()

    _finish(f, h_ref, g_ref, gn_ref, obuf.at[slot], None)
    plane_copy(i, slot).start()

    @pl.when(i == n - 1)
    def _():
        plane_copy(i - 1, 1 - slot).wait()
        plane_copy(i, slot).wait()


def _down_proj(a, w, h, g, g_next, frames, tm):
    t, kk = a.shape
    d = w.shape[1]
    n_tiles = t // tm
    tok = pl.BlockSpec((tm, d), lambda i: (i, 0))
    vec = pl.BlockSpec((1, d), lambda i: (0, 0))
    if frames is None:
        out_specs = [tok, tok]
        out_shape = [jax.ShapeDtypeStruct((t, d), F32), jax.ShapeDtypeStruct((t, d), BF16)]
        scratch = []
    else:
        planes, rows = frames
        assert planes * rows == t and rows % tm == 0 and n_tiles >= 2
        out_specs = [pl.BlockSpec(memory_space=pl.ANY)]
        out_shape = [jax.ShapeDtypeStruct((rows // tm, tm, planes, d), F32)]
        scratch = [pltpu.VMEM((2, tm, d), F32), pltpu.SemaphoreType.DMA((2,))]
    res = pl.pallas_call(
        functools.partial(_down_kernel, to_frames=frames is not None),
        grid=(n_tiles,),
        in_specs=[pl.BlockSpec((tm, kk), lambda i: (i, 0)),
                  pl.BlockSpec((kk, d), lambda i: (0, 0), **_RESIDENT),
                  tok, vec, vec],
        out_specs=out_specs,
        out_shape=out_shape,
        scratch_shapes=scratch,
        compiler_params=_params("arbitrary", vmem=VMEM_LIMIT_LARGE),
        name="down_proj",
    )(a, w, h, g, g_next)
    return res if frames is None else (res[0], None)


def _sgu_down_kernel(u_ref, v_ref, lg_ref, lb_ref, ws_ref, bs_ref, w_ref,
                     h_ref, g_ref, gn_ref, o_ref, hn_ref, gate_ref, *, heads_per_group):
    tc, mt, half = u_ref.shape
    heads = ws_ref.shape[0]
    hd = half // heads
    cpb = SGU_BLOCK // tc
    cps = SEQ_CHUNK // tc

    row = lax.broadcasted_iota(jnp.int32, (SGU_BLOCK, SGU_BLOCK), 0)
    col = lax.broadcasted_iota(jnp.int32, (SGU_BLOCK, SGU_BLOCK), 1)
    causal = (col % cpb) // cps <= (row % cpb) // cps
    s1 = jnp.zeros((tc, mt, LANES), F32)
    s2 = jnp.zeros((tc, mt, LANES), F32)
    for c in range(half // LANES):
        vc = v_ref[:, :, pl.ds(c * LANES, LANES)].astype(F32)
        s1 += vc
        s2 += vc * vc
    mean = jnp.sum(s1, axis=-1, keepdims=True) * (1.0 / half)
    rstd = lax.rsqrt(jnp.sum(s2, axis=-1, keepdims=True) * (1.0 / half) - mean * mean + EPS)

    def block_of(x, n):
        return jnp.concatenate([x[o, n * cpb:(n + 1) * cpb] for o in range(tc)], axis=0)

    f = None
    for hh in range(heads):
        cols = pl.ds(hh * hd, hd)
        w = jnp.where(causal, ws_ref[hh], 0.0).astype(BF16)
        lg = lg_ref[:, cols]
        lb = lb_ref[:, cols]
        bias = bs_ref[hh]
        for n in range(mt // cpb):
            rows = pl.ds(n * cpb, cpb)
            v = jnp.concatenate([v_ref[o, rows, cols] for o in range(tc)], axis=0).astype(F32)
            vn = ((v - block_of(mean, n)) * block_of(rstd, n) * lg + lb).astype(BF16)
            s = _dot(w, vn) + bias
            u = jnp.concatenate([u_ref[o, rows, cols] for o in range(tc)], axis=0).astype(F32)
            gated = (u * s).astype(BF16)
            for o in range(tc):
                gate_ref[o, rows, cols] = gated[o * cpb:(o + 1) * cpb]
        if (hh + 1) % heads_per_group == 0:
            gcols = pl.ds((hh + 1 - heads_per_group) * hd, heads_per_group * hd)
            part = _dot(gate_ref[:, :, gcols].reshape(tc * mt, heads_per_group * hd),
                        w_ref[gcols, :])
            f = part if f is None else f + part
    _finish(f, h_ref, g_ref, gn_ref, o_ref, hn_ref)


def _sgu_down(z, ln_g, ln_b, w_s, b_s, w_out, h, g, g_next, mt, heads_per_group):
    tc, m, n = z.shape
    half = n // 2
    heads = w_s.shape[0]
    assert heads % heads_per_group == 0
    d = w_out.shape[1]
    tok = lambda c: pl.BlockSpec((tc, mt, c), lambda i: (0, i, 0))
    const = lambda *shape: pl.BlockSpec(shape, lambda i: (0,) * len(shape), **_RESIDENT)
    return pl.pallas_call(
        functools.partial(_sgu_down_kernel, heads_per_group=heads_per_group),
        grid=(m // mt,),
        in_specs=[pl.BlockSpec((tc, mt, half), lambda i: (0, i, 0)),
                  pl.BlockSpec((tc, mt, half), lambda i: (0, i, 1)),
                  const(1, half), const(1, half),
                  const(heads, SGU_BLOCK, SGU_BLOCK), const(heads, SGU_BLOCK, 1),
                  const(half, d),
                  tok(d), const(1, d), const(1, d)],
        out_specs=[tok(d), tok(d)],
        out_shape=[jax.ShapeDtypeStruct((tc, m, d), F32), jax.ShapeDtypeStruct((tc, m, d), BF16)],
        scratch_shapes=[pltpu.VMEM((tc, mt, half), BF16)],
        compiler_params=_params("parallel", vmem=VMEM_LIMIT_LARGE),
        name="sgu_down",
    )(z, z, ln_g, ln_b, w_s, b_s, w_out, h, g, g_next)


def _zoh(a_re, a_im, log_dt):
    lam_re = jnp.minimum(a_re, EIG_CLIP)
    lam_im = a_im
    dt = jnp.exp(log_dt)
    mag = jnp.exp(lam_re * dt)
    ab_re = mag * jnp.cos(lam_im * dt)
    ab_im = mag * jnp.sin(lam_im * dt)
    denom = lam_re * lam_re + lam_im * lam_im
    coef_re = ((ab_re - 1.0) * lam_re + ab_im * lam_im) / denom
    coef_im = (ab_im * lam_re - (ab_re - 1.0) * lam_im) / denom
    return ab_re, ab_im, coef_re, coef_im


def _cmul(x_re, x_im, y_re, y_im):
    return x_re * y_re - x_im * y_im, x_re * y_im + x_im * y_re


def _dot_3pass(a, b):
    a_hi = a.astype(BF16)
    a_lo = (a - a_hi.astype(F32)).astype(BF16)
    b_hi = b.astype(BF16)
    b_lo = (b - b_hi.astype(F32)).astype(BF16)
    return _dot(a_hi, b_hi) + _dot(a_hi, b_lo) + _dot(a_lo, b_hi)


def _s5_tables_kernel(are_ref, aim_ref, ldt_ref, bc1_ref, bc2_ref, cc_ref,
                      areb_ref, aimb_ref, ldtb_ref,
                      mi_ref, ms_ref, mc_ref, apr_ref, api_ref):
    tc, gl, gc, half = SSM_CHUNK, GROUPS_PER_TILE, S5_GROUP, LANES // 2
    lane = lax.broadcasted_iota(jnp.int32, (gc, LANES), 1)
    lower = lane < half
    sgn = jnp.where(lower, -1.0, 1.0)

    ab_re, ab_im, x_re, x_im = _zoh(are_ref[...], aim_ref[...], ldt_ref[...])
    bc1 = bc1_ref[...]
    bc2 = bc2_ref[...]
    ms_ref[...] = jnp.zeros(ms_ref.shape, ms_ref.dtype)
    w_lags = []
    for d in range(tc):
        w = x_re * bc1 + x_im * (sgn * bc2)
        wsw = x_re * bc2 - x_im * (sgn * bc1)
        w_lags.append(w)
        k = tc - 1 - d
        for g in range(gl):
            keep = lower if g % 2 == 0 else jnp.logical_not(lower)
            re_src, im_src = (w[g], wsw[g]) if g % 2 == 0 else (wsw[g], w[g])
            rows = pl.ds(k * LANES + g * gc, gc)
            ms_ref[rows, pl.ds((g // 2) * LANES, LANES)] = jnp.where(keep, re_src, 0.0).astype(BF16)
            ms_ref[rows, pl.ds((gl // 2 + g // 2) * LANES, LANES)] = (
                jnp.where(keep, im_src, 0.0).astype(BF16))
        x_re, x_im = _cmul(x_re, x_im, ab_re, ab_im)

    pw_re, pw_im = ab_re, ab_im
    for _ in range(tc - 1):
        pw_re, pw_im = _cmul(pw_re, pw_im, ab_re, ab_im)
    apr_ref[...] = pw_re
    api_ref[...] = pw_im

    row = lax.broadcasted_iota(jnp.int32, (LANES, LANES), 0)
    lane_sq = lax.broadcasted_iota(jnp.int32, (LANES, LANES), 1)
    c_sign = jnp.where(row < half, 1.0, -1.0)
    k_rows = []
    for g in range(gl):
        w_stack = jnp.concatenate([w_lags[d][g] for d in range(tc)], axis=0)
        k_g = _dot_3pass(w_stack, cc_ref[g] * c_sign)
        k_rows.append(jnp.where(lane_sq // gc == g, k_g, 0.0))
    zero = jnp.zeros((LANES, LANES), BF16)
    lag_blocks = [jnp.concatenate([k_rows[g][d * gc:(d + 1) * gc] for g in range(gl)],
                                  axis=0).astype(BF16) for d in range(tc)]
    for k in range(tc):
        for kp in range(tc):
            mi_ref[pl.ds(k * LANES, LANES), pl.ds(kp * LANES, LANES)] = (
                lag_blocks[kp - k] if kp >= k else zero)

    ab_re, ab_im, _, _ = _zoh(areb_ref[...], aimb_ref[...], ldtb_ref[...])
    c_re = cc_ref[:, :half, :]
    c_im = cc_ref[:, half:, :]
    grp = lax.broadcasted_iota(jnp.int32, (gl, half, LANES), 0)
    lane3 = lax.broadcasted_iota(jnp.int32, (gl, half, LANES), 2)
    diag = lane3 // gc == grp
    q_re, q_im = ab_re, ab_im
    for kp in range(tc):
        cm_re = jnp.where(diag, c_re * q_re - c_im * q_im, 0.0).astype(BF16)
        cm_im = jnp.where(diag, -(c_re * q_im + c_im * q_re), 0.0).astype(BF16)
        for g in range(gl):
            mc_ref[pl.ds(g * half, half), pl.ds(kp * LANES, LANES)] = cm_re[g]
            mc_ref[pl.ds((gl + g) * half, half), pl.ds(kp * LANES, LANES)] = cm_im[g]
        if kp + 1 < tc:
            q_re, q_im = _cmul(q_re, q_im, ab_re, ab_im)


def _s5_tables(a_re, a_im, log_dt, b_re, b_im, c_re, c_im):
    g, p = a_re.shape
    gc = b_re.shape[-1]
    gl = GROUPS_PER_TILE
    assert 2 * p == LANES and gc == S5_GROUP and g % gl == 0
    nj = g // gl
    kk = SSM_CHUNK * LANES
    ns = 2 * gl * p
    are_t = jnp.concatenate([a_re, a_re], axis=-1).reshape(g, 1, LANES)
    aim_t = jnp.concatenate([a_im, a_im], axis=-1).reshape(g, 1, LANES)
    ldt_t = jnp.broadcast_to(log_dt[:, None, None], (g, 1, LANES))
    bt_re = jnp.swapaxes(b_re, 1, 2)
    bt_im = jnp.swapaxes(b_im, 1, 2)
    bc1 = jnp.concatenate([bt_re, bt_im], axis=-1)
    bc2 = jnp.concatenate([bt_im, bt_re], axis=-1)
    ct = jnp.concatenate([jnp.swapaxes(c_re, 1, 2), jnp.swapaxes(c_im, 1, 2)], axis=1)
    cc = jnp.tile(ct, (1, 1, gl))
    lane_blk = lambda r, c: pl.BlockSpec((gl, r, c), lambda j: (j, 0, 0))
    mi, ms, mc, apr, api = pl.pallas_call(
        _s5_tables_kernel,
        grid=(nj,),
        in_specs=[lane_blk(1, LANES), lane_blk(1, LANES), lane_blk(1, LANES),
                  lane_blk(gc, LANES), lane_blk(gc, LANES), lane_blk(2 * p, LANES),
                  lane_blk(p, 1), lane_blk(p, 1), lane_blk(1, 1)],
        out_specs=[pl.BlockSpec((None, kk, kk), lambda j: (j, 0, 0)),
                   pl.BlockSpec((None, kk, ns), lambda j: (j, 0, 0)),
                   pl.BlockSpec((None, ns, kk), lambda j: (j, 0, 0)),
                   lane_blk(1, LANES), lane_blk(1, LANES)],
        out_shape=[jax.ShapeDtypeStruct((nj, kk, kk), BF16),
                   jax.ShapeDtypeStruct((nj, kk, ns), BF16),
                   jax.ShapeDtypeStruct((nj, ns, kk), BF16),
                   jax.ShapeDtypeStruct((g, 1, LANES), F32),
                   jax.ShapeDtypeStruct((g, 1, LANES), F32)],
        compiler_params=_params("parallel"),
        name="s5_tables",
    )(are_t, aim_t, ldt_t, bc1, bc2, cc, a_re[:, :, None], a_im[:, :, None],
      log_dt[:, None, None])
    pa_re = apr[:, 0, :p].reshape(1, g * p)
    pa_im = api[:, 0, :p].reshape(1, g * p)
    return mi, ms, mc, pa_re, pa_im


def _s5_in_kernel(x_ref, g_ref, w_ref, cg_ref, co_ref, u_ref, *rest, from_frames):
    k, r = pl.program_id(0), pl.program_id(1)
    nr = pl.num_programs(1)
    step = k * nr + r

    if from_frames:
        hp_ref, cgo_ref, coo_ref, wb_ref, xbuf, sem = rest

        def plane_copy(s, slot):
            return pltpu.make_async_copy(x_ref.at[s % nr, :, s // nr, :], xbuf.at[slot],
                                         sem.at[slot])

        @pl.when(step == 0)
        def _():
            plane_copy(step, 0).start()

        @pl.when(step + 1 < pl.num_programs(0) * nr)
        def _():
            plane_copy(step + 1, (step + 1) % 2).start()

        plane_copy(step, step % 2).wait()
    else:
        cgo_ref, coo_ref, wb_ref = rest

    @pl.when(step == 0)
    def _():
        wb_ref[...] = w_ref[...].astype(BF16)

    if from_frames:
        x = xbuf[step % 2]
        hp_ref[...] = x
        hn = _rms(x, g_ref[...]).astype(BF16)
    else:
        hn = x_ref[...]
    u_ref[...] = _dot(hn, wb_ref[...]).astype(u_ref.dtype)
    cgo_ref[...] = cg_ref[...].astype(BF16)
    coo_ref[...] = co_ref[...].astype(BF16)


def _s5_in(x, g, w, w_glu, w_out, layer, from_frames, tr):
    if from_frames:
        nr, _, tc, d = x.shape
        m = nr * tr
    else:
        tc, m, d = x.shape
        nr = m // tr
    cr = _cast_rows(w_glu, tc * nr)
    tok = pl.BlockSpec((None, tr, d), lambda k, r: (k, r, 0))
    cast_in = pl.BlockSpec((None, cr, d), lambda k, r: (layer, k * nr + r, 0))
    cast_out = pl.BlockSpec((cr, d), lambda k, r: (k * nr + r, 0))
    casts = [jax.ShapeDtypeStruct((d, d), BF16)] * 2
    out_specs = [tok, cast_out, cast_out]
    out_shape = [jax.ShapeDtypeStruct((tc, m, d), BF16)] + casts
    scratch = [pltpu.VMEM((d, d), BF16)]
    if from_frames:
        out_specs.insert(1, tok)
        out_shape.insert(1, jax.ShapeDtypeStruct((tc, m, d), F32))
        scratch += [pltpu.VMEM((2, tr, d), F32), pltpu.SemaphoreType.DMA((2,))]
    res = pl.pallas_call(
        functools.partial(_s5_in_kernel, from_frames=from_frames),
        grid=(tc, nr),
        in_specs=[pl.BlockSpec(memory_space=pl.ANY) if from_frames else tok,
                  pl.BlockSpec((1, d), lambda k, r: (0, 0)),
                  pl.BlockSpec((None, d, d), lambda k, r: (layer, 0, 0), **_RESIDENT),
                  cast_in, cast_in],
        out_specs=out_specs,
        out_shape=out_shape,
        scratch_shapes=scratch,
        compiler_params=_params("arbitrary", "arbitrary"),
        name="s5_in",
    )(x, g, w, w_glu, w_out)
    return res if from_frames else (res[0], None, res[1], res[2])


def _chunk_rows(u_ref):
    return jnp.concatenate([u_ref[k] for k in range(SSM_CHUNK)], axis=-1)


def _s5_state_kernel(u_ref, m_ref, zre_ref, zim_ref):
    z = _dot(_chunk_rows(u_ref), m_ref[...])
    half = z.shape[1] // 2
    zre_ref[...] = z[:, :half]
    zim_ref[...] = z[:, half:]


def _s5_state_in(u, m_state):
    tc, m, d = u.shape
    nj, kk, ns = m_state.shape
    half = ns // 2
    return pl.pallas_call(
        _s5_state_kernel,
        grid=(nj,),
        in_specs=[pl.BlockSpec((tc, m, LANES), lambda j: (0, 0, j)),
                  pl.BlockSpec((None, kk, ns), lambda j: (j, 0, 0))],
        out_specs=[pl.BlockSpec((m, half), lambda j: (0, j)),
                   pl.BlockSpec((m, half), lambda j: (0, j))],
        out_shape=[jax.ShapeDtypeStruct((m, nj * half), F32)] * 2,
        compiler_params=_params("parallel"),
        name="s5_state_in",
    )(u, m_state)


def _s5_scan_kernel(zre_ref, zim_ref, are_ref, aim_ref, sre_ref, sim_ref, st_re, st_im):
    rows = zre_ref.shape[0]

    @pl.when(pl.program_id(1) == 0)
    def _():
        st_re[...] = jnp.zeros_like(st_re)
        st_im[...] = jnp.zeros_like(st_im)

    a_re = are_ref[...]
    a_im = aim_ref[...]

    def step(i, carry):
        s_re, s_im = carry
        row = pl.ds(i, 1)
        sre_ref[row, :] = s_re
        sim_ref[row, :] = s_im
        n_re = a_re * s_re - a_im * s_im + zre_ref[row, :]
        n_im = a_re * s_im + a_im * s_re + zim_ref[row, :]
        return n_re, n_im

    s_re, s_im = lax.fori_loop(0, rows, step, (st_re[...], st_im[...]))
    st_re[...] = s_re
    st_im[...] = s_im


def _s5_scan(z_re, z_im, a_re, a_im, rows_per_seq, tr):
    m, n = z_re.shape
    blk = pl.BlockSpec((tr, n), lambda s, r: (s * (rows_per_seq // tr) + r, 0))
    coef = pl.BlockSpec((1, n), lambda s, r: (0, 0))
    return pl.pallas_call(
        _s5_scan_kernel,
        grid=(m // rows_per_seq, rows_per_seq // tr),
        in_specs=[blk, blk, coef, coef],
        out_specs=[blk, blk],
        out_shape=[jax.ShapeDtypeStruct((m, n), F32)] * 2,
        scratch_shapes=[pltpu.VMEM((1, n), F32), pltpu.VMEM((1, n), F32)],
        compiler_params=_params("arbitrary", "arbitrary"),
        name="s5_scan",
    )(z_re, z_im, a_re, a_im)


def _s5_mix_kernel(u_ref, sre_ref, sim_ref, mi_ref, mc_ref, y_ref):
    x = _chunk_rows(u_ref)
    s = jnp.concatenate([sre_ref[...], sim_ref[...]], axis=-1).astype(BF16)
    kk = mi_ref.shape[0]
    step = 2 * LANES
    for c0 in range(0, kk, step):
        cols = pl.ds(c0, step)
        y = _dot(x[:, :c0 + step], mi_ref[pl.ds(0, c0 + step), cols]) + _dot(s, mc_ref[:, cols])
        y_ref[:, cols] = y.astype(y_ref.dtype)


def _s5_mix(u, s_re, s_im, m_intra, m_carry):
    tc, m, d = u.shape
    nj, kk, _ = m_intra.shape
    ns = m_carry.shape[1]
    half = ns // 2
    return pl.pallas_call(
        _s5_mix_kernel,
        grid=(nj,),
        in_specs=[pl.BlockSpec((tc, m, LANES), lambda j: (0, 0, j)),
                  pl.BlockSpec((m, half), lambda j: (0, j)),
                  pl.BlockSpec((m, half), lambda j: (0, j)),
                  pl.BlockSpec((None, kk, kk), lambda j: (j, 0, 0)),
                  pl.BlockSpec((None, ns, kk), lambda j: (j, 0, 0))],
        out_specs=pl.BlockSpec((None, m, kk), lambda j: (j, 0, 0)),
        out_shape=jax.ShapeDtypeStruct((nj, m, kk), BF16),
        compiler_params=_params("parallel"),
        name="s5_mix",
    )(u, s_re, s_im, m_intra, m_carry)


def _s5_out_kernel(y_ref, u_ref, h_ref, dsk_ref, wg_ref, wo_ref, g_ref, gn_ref, o_ref, hn_ref,
                   *, row_piece):
    nj, tr, _ = y_ref.shape
    for r0 in range(0, tr, row_piece):
        rows = pl.ds(r0, row_piece)
        y = (jnp.concatenate([y_ref[j, rows, :] for j in range(nj)], axis=-1).astype(F32)
             + dsk_ref[...] * u_ref[rows, :].astype(F32))
        z = _gelu(y)
        gate = jax.nn.sigmoid(_dot(z.astype(BF16), wg_ref[...]))
        m = _dot((z * gate).astype(BF16), wo_ref[...])
        _finish(m, h_ref, g_ref, gn_ref, o_ref, hn_ref, rows)


def _s5_out(y, u, h, d_skip, w_glu, w_out, g, g_next, tr):
    nj, m, _ = y.shape
    tc, _, d = h.shape
    tok = pl.BlockSpec((None, tr, d), lambda k, r: (k, r, 0))
    vec = pl.BlockSpec((1, d), lambda k, r: (0, 0))
    mat = pl.BlockSpec((d, d), lambda k, r: (0, 0), **_RESIDENT)
    return pl.pallas_call(
        functools.partial(_s5_out_kernel, row_piece=_tile(tr, 256)),
        grid=(tc, m // tr),
        in_specs=[pl.BlockSpec((nj, tr, LANES), lambda k, r: (0, r, k)),
                  tok, tok, vec, mat, mat, vec, vec],
        out_specs=[tok, tok],
        out_shape=[jax.ShapeDtypeStruct(h.shape, F32), jax.ShapeDtypeStruct(h.shape, BF16)],
        compiler_params=_params("parallel", "parallel"),
        name="s5_out",
    )(y, u, h, d_skip, w_glu, w_out, g, g_next)


def _s5_layer(h, hn, layer, g_pre, g_post, g_next, w_in, a_re, a_im, log_dt, b_re, b_im,
              c_re, c_im, d_skip, w_glu, w_out, rows_per_seq):
    m_intra, m_state, m_carry, pa_re, pa_im = _s5_tables(a_re, a_im, log_dt, b_re, b_im,
                                                         c_re, c_im)
    r = rows_per_seq
    if hn is None:
        u, h, w_glu, w_out = _s5_in(h, g_pre, w_in, w_glu, w_out, layer, True, h.shape[1])
    else:
        u, _, w_glu, w_out = _s5_in(hn, g_pre, w_in, w_glu, w_out, layer, False,
                                    _tile(h.shape[1], 1024))
    z_re, z_im = _s5_state_in(u, m_state)
    s_re, s_im = _s5_scan(z_re, z_im, pa_re, pa_im, r, _tile(r, 64))
    y = _s5_mix(u, s_re, s_im, m_intra, m_carry)
    return _s5_out(y, u, h, d_skip, w_glu, w_out, g_post, g_next, _tile(r, 512))


def _frames_by_offset(w, axes):
    cpb = SGU_BLOCK // SSM_CHUNK
    for ax in axes:
        shape = w.shape
        w = w.reshape(shape[:ax] + (cpb, SSM_CHUNK) + shape[ax + 1:])
        w = jnp.swapaxes(w, ax, ax + 1).reshape(shape)
    return w


def kernel(x, norm_g, s5_w_in, s5_a_re, s5_a_im, s5_log_dt, s5_b_re, s5_b_im, s5_c_re, s5_c_im,
           s5_d, s5_w_glu, s5_w_out, sgu_w_in, sgu_ln_g, sgu_ln_b, sgu_w_s, sgu_b_s, sgu_w_out,
           ffn_w_up, ffn_w_down):
    bsz, seq, d = x.shape
    t = bsz * seq
    tc = SSM_CHUNK
    r = seq // tc
    m = bsz * r
    depth = norm_g.shape[0]
    assert seq % SGU_BLOCK == 0 and d % LANES == 0 and SEQ_CHUNK % tc == 0
    gains = norm_g.reshape(depth, 4, 1, d)
    tr0 = _tile(r, 512)
    h = x.reshape(m // tr0, tr0, tc, d)
    hn = None
    for i in range(depth):
        g = gains[i]
        j = i // 2
        if i % 2 == 0:
            h, hn = _s5_layer(h, hn, j, g[0], g[1], g[2], s5_w_in, s5_a_re[j], s5_a_im[j],
                              s5_log_dt[j], s5_b_re[j], s5_b_im[j], s5_c_re[j], s5_c_im[j],
                              s5_d[j].reshape(1, d), s5_w_glu, s5_w_out, r)
        else:
            half = sgu_ln_g.shape[-1]
            heads = sgu_w_s.shape[1]
            z, w_out = _up_proj(hn.reshape(t, d), sgu_w_in, sgu_w_out, j, _gelu,
                                _tile(t, 2048), _tile(half, 1024))
            h, hn = _sgu_down(z.reshape(tc, m, 2 * half), sgu_ln_g[j].reshape(1, half),
                              sgu_ln_b[j].reshape(1, half), _frames_by_offset(sgu_w_s[j], (1, 2)),
                              _frames_by_offset(sgu_b_s[j], (1,)).reshape(heads, SGU_BLOCK, 1),
                              w_out, h, g[1], g[2], _tile(r, 32), _tile(heads, 2))
        a, w_down = _up_proj(hn.reshape(t, d), ffn_w_up, ffn_w_down, i, _relu2,
                             _tile(t, 2048), _tile(ffn_w_up.shape[-1], 1024))
        if i == depth - 1:
            out, _ = _down_proj(a, w_down, h.reshape(t, d), g[3], g[3], (tc, m), _tile(m, 256))
            return out.reshape(bsz, seq, d)
        h, hn = _down_proj(a, w_down, h.reshape(t, d), g[3], gains[i + 1][0], None, _tile(t, 256))
        h = h.reshape(tc, m, d)
        hn = hn.reshape(tc, m, d)
```

```python
import functools
import math

import jax
import jax.numpy as jnp
from jax import lax
from jax.experimental import pallas as pl
from jax.experimental.pallas import tpu as pltpu

F32 = jnp.float32
BF16 = jnp.bfloat16

EPS = 1e-6
EIG_CLIP = -1e-4
SEQ_CHUNK = 64
SGU_BLOCK = 128
S5_GROUP = 16
SSM_CHUNK = 8
LANES = 128
GROUPS_PER_TILE = LANES // S5_GROUP
VMEM_LIMIT = 56 * 1024 * 1024
VMEM_LIMIT_LARGE = 60 * 1024 * 1024

_dot = functools.partial(jnp.dot, preferred_element_type=F32)
_RESIDENT = dict(pipeline_mode=pl.Buffered(1))


def _params(*sem, vmem=VMEM_LIMIT):
    return pltpu.CompilerParams(dimension_semantics=sem, vmem_limit_bytes=vmem)


def _rms(x, g):
    ms = jnp.mean(x * x, axis=-1, keepdims=True)
    return x * lax.rsqrt(ms + EPS) * g


def _relu2(y):
    a = jnp.maximum(y, 0.0)
    return a * a


_GELU_C1 = math.sqrt(2.0 / math.pi)
_GELU_C2 = _GELU_C1 * 0.044715


def _gelu(x):
    hx = 0.5 * x
    return hx + hx * jnp.tanh(x * (_GELU_C1 + _GELU_C2 * (x * x)))


def _tile(n, pref):
    t = min(n, pref)
    assert n % t == 0, (n, t)
    return t


def _up_kernel(x_ref, w_ref, cw_ref, o_ref, cwo_ref, wb_ref, *, act, row_piece):
    @pl.when(pl.program_id(1) == 0)
    def _():
        wb_ref[...] = w_ref[...].astype(BF16)

    tm = x_ref.shape[0]
    for r0 in range(0, tm, row_piece):
        rows = pl.ds(r0, row_piece)
        o_ref[rows, :] = act(_dot(x_ref[rows, :], wb_ref[...])).astype(o_ref.dtype)
    cwo_ref[...] = cw_ref[...].astype(BF16)


def _cast_rows(cast_w, steps):
    rows = cast_w.shape[1] // steps
    assert rows * steps == cast_w.shape[1] and rows % 16 == 0
    return rows


def _up_proj(hn, w, cast_w, layer, act, tm, tn):
    t, d = hn.shape
    n = w.shape[2]
    nm = t // tm
    cr = _cast_rows(cast_w, (n // tn) * nm)
    cc = cast_w.shape[2]
    return pl.pallas_call(
        functools.partial(_up_kernel, act=act, row_piece=_tile(tm, 1024)),
        grid=(n // tn, nm),
        in_specs=[pl.BlockSpec((tm, d), lambda j, i: (i, 0)),
                  pl.BlockSpec((None, d, tn), lambda j, i: (layer, 0, j)),
                  pl.BlockSpec((None, cr, cc), lambda j, i: (layer, j * nm + i, 0))],
        out_specs=[pl.BlockSpec((tm, tn), lambda j, i: (i, j)),
                   pl.BlockSpec((cr, cc), lambda j, i: (j * nm + i, 0))],
        out_shape=[jax.ShapeDtypeStruct((t, n), BF16),
                   jax.ShapeDtypeStruct(cast_w.shape[1:], BF16)],
        scratch_shapes=[pltpu.VMEM((d, tn), BF16)],
        compiler_params=_params("arbitrary", "arbitrary", vmem=VMEM_LIMIT_LARGE),
        name="up_proj",
    )(hn, w, cast_w)


def _finish(f, h_ref, g_ref, gn_ref, o_ref, hn_ref, rows=None):
    if rows is None:
        h = h_ref[...].reshape(f.shape) + _rms(f, g_ref[...])
        o_ref[...] = h.reshape(o_ref.shape)
        if hn_ref is not None:
            hn_ref[...] = _rms(h, gn_ref[...]).astype(BF16).reshape(hn_ref.shape)
    else:
        h = h_ref[rows, :] + _rms(f, g_ref[...])
        o_ref[rows, :] = h
        if hn_ref is not None:
            hn_ref[rows, :] = _rms(h, gn_ref[...]).astype(BF16)


def _down_kernel(a_ref, w_ref, h_ref, g_ref, gn_ref, o_ref, *rest, to_frames):
    f = _dot(a_ref[...], w_ref[...])
    if not to_frames:
        _finish(f, h_ref, g_ref, gn_ref, o_ref, rest[0])
        return
    obuf, sem = rest
    i = pl.program_id(0)
    n = pl.num_programs(0)
    row_tiles = o_ref.shape[0]

    def plane_copy(s, slot):
        return pltpu.make_async_copy(obuf.at[slot], o_ref.at[s % row_tiles, :, s // row_tiles, :],
                                     sem.at[slot])

    slot = i % 2

    @pl.when(i >= 2)
    def _():
        plane_copy(i - 2, slot).wait()

    _finish(f, h_ref, g_ref, gn_ref, obuf.at[slot], None)
    plane_copy(i, slot).start()

    @pl.when(i == n - 1)
    def _():
        plane_copy(i - 1, 1 - slot).wait()
        plane_copy(i, slot).wait()


def _down_proj(a, w, h, g, g_next, frames, tm):
    t, kk = a.shape
    d = w.shape[1]
    n_tiles = t // tm
    tok = pl.BlockSpec((tm, d), lambda i: (i, 0))
    vec = pl.BlockSpec((1, d), lambda i: (0, 0))
    if frames is None:
        out_specs = [tok, tok]
        out_shape = [jax.ShapeDtypeStruct((t, d), F32), jax.ShapeDtypeStruct((t, d), BF16)]
        scratch = []
    else:
        planes, rows = frames
        assert planes * rows == t and rows % tm == 0 and n_tiles >= 2
        out_specs = [pl.BlockSpec(memory_space=pl.ANY)]
        out_shape = [jax.ShapeDtypeStruct((rows // tm, tm, planes, d), F32)]
        scratch = [pltpu.VMEM((2, tm, d), F32), pltpu.SemaphoreType.DMA((2,))]
    res = pl.pallas_call(
        functools.partial(_down_kernel, to_frames=frames is not None),
        grid=(n_tiles,),
        in_specs=[pl.BlockSpec((tm, kk), lambda i: (i, 0)),
                  pl.BlockSpec((kk, d), lambda i: (0, 0), **_RESIDENT),
                  tok, vec, vec],
        out_specs=out_specs,
        out_shape=out_shape,
        scratch_shapes=scratch,
        compiler_params=_params("arbitrary", vmem=VMEM_LIMIT_LARGE),
        name="down_proj",
    )(a, w, h, g, g_next)
    return res if frames is None else (res[0], None)


def _sgu_down_kernel(u_ref, v_ref, lg_ref, lb_ref, ws_ref, bs_ref, w_ref,
                     h_ref, g_ref, gn_ref, o_ref, hn_ref, gate_ref, *, heads_per_group):
    tc, mt, half = u_ref.shape
    heads = ws_ref.shape[0]
    hd = half // heads
    cpb = SGU_BLOCK // tc
    cps = SEQ_CHUNK // tc

    row = lax.broadcasted_iota(jnp.int32, (SGU_BLOCK, SGU_BLOCK), 0)
    col = lax.broadcasted_iota(jnp.int32, (SGU_BLOCK, SGU_BLOCK), 1)
    causal = (col % cpb) // cps <= (row % cpb) // cps
    s1 = jnp.zeros((tc, mt, LANES), F32)
    s2 = jnp.zeros((tc, mt, LANES), F32)
    for c in range(half // LANES):
        vc = v_ref[:, :, pl.ds(c * LANES, LANES)].astype(F32)
        s1 += vc
        s2 += vc * vc
    mean = jnp.sum(s1, axis=-1, keepdims=True) * (1.0 / half)
    rstd = lax.rsqrt(jnp.sum(s2, axis=-1, keepdims=True) * (1.0 / half) - mean * mean + EPS)

    def block_of(x, n):
        return jnp.concatenate([x[o, n * cpb:(n + 1) * cpb] for o in range(tc)], axis=0)

    f = None
    for hh in range(heads):
        cols = pl.ds(hh * hd, hd)
        w = jnp.where(causal, ws_ref[hh], 0.0).astype(BF16)
        lg = lg_ref[:, cols]
        lb = lb_ref[:, cols]
        bias = bs_ref[hh]
        for n in range(mt // cpb):
            rows = pl.ds(n * cpb, cpb)
            v = jnp.concatenate([v_ref[o, rows, cols] for o in range(tc)], axis=0).astype(F32)
            vn = ((v - block_of(mean, n)) * block_of(rstd, n) * lg + lb).astype(BF16)
            s = _dot(w, vn) + bias
            u = jnp.concatenate([u_ref[o, rows, cols] for o in range(tc)], axis=0).astype(F32)
            gated = (u * s).astype(BF16)
            for o in range(tc):
                gate_ref[o, rows, cols] = gated[o * cpb:(o + 1) * cpb]
        if (hh + 1) % heads_per_group == 0:
            gcols = pl.ds((hh + 1 - heads_per_group) * hd, heads_per_group * hd)
            part = _dot(gate_ref[:, :, gcols].reshape(tc * mt, heads_per_group * hd),
                        w_ref[gcols, :])
            f = part if f is None else f + part
    _finish(f, h_ref, g_ref, gn_ref, o_ref, hn_ref)


def _sgu_down(z, ln_g, ln_b, w_s, b_s, w_out, h, g, g_next, mt, heads_per_group):
    tc, m, n = z.shape
    half = n // 2
    heads = w_s.shape[0]
    assert heads % heads_per_group == 0
    d = w_out.shape[1]
    tok = lambda c: pl.BlockSpec((tc, mt, c), lambda i: (0, i, 0))
    const = lambda *shape: pl.BlockSpec(shape, lambda i: (0,) * len(shape), **_RESIDENT)
    return pl.pallas_call(
        functools.partial(_sgu_down_kernel, heads_per_group=heads_per_group),
        grid=(m // mt,),
        in_specs=[pl.BlockSpec((tc, mt, half), lambda i: (0, i, 0)),
                  pl.BlockSpec((tc, mt, half), lambda i: (0, i, 1)),
                  const(1, half), const(1, half),
                  const(heads, SGU_BLOCK, SGU_BLOCK), const(heads, SGU_BLOCK, 1),
                  const(half, d),
                  tok(d), const(1, d), const(1, d)],
        out_specs=[tok(d), tok(d)],
        out_shape=[jax.ShapeDtypeStruct((tc, m, d), F32), jax.ShapeDtypeStruct((tc, m, d), BF16)],
        scratch_shapes=[pltpu.VMEM((tc, mt, half), BF16)],
        compiler_params=_params("parallel", vmem=VMEM_LIMIT_LARGE),
        name="sgu_down",
    )(z, z, ln_g, ln_b, w_s, b_s, w_out, h, g, g_next)


def _zoh(a_re, a_im, log_dt):
    lam_re = jnp.minimum(a_re, EIG_CLIP)
    lam_im = a_im
    dt = jnp.exp(log_dt)
    mag = jnp.exp(lam_re * dt)
    ab_re = mag * jnp.cos(lam_im * dt)
    ab_im = mag * jnp.sin(lam_im * dt)
    denom = lam_re * lam_re + lam_im * lam_im
    coef_re = ((ab_re - 1.0) * lam_re + ab_im * lam_im) / denom
    coef_im = (ab_im * lam_re - (ab_re - 1.0) * lam_im) / denom
    return ab_re, ab_im, coef_re, coef_im


def _cmul(x_re, x_im, y_re, y_im):
    return x_re * y_re - x_im * y_im, x_re * y_im + x_im * y_re


def _dot_3pass(a, b):
    a_hi = a.astype(BF16)
    a_lo = (a - a_hi.astype(F32)).astype(BF16)
    b_hi = b.astype(BF16)
    b_lo = (b - b_hi.astype(F32)).astype(BF16)
    return _dot(a_hi, b_hi) + _dot(a_hi, b_lo) + _dot(a_lo, b_hi)


def _s5_tables_kernel(are_ref, aim_ref, ldt_ref, bc1_ref, bc2_ref, cc_ref,
                      areb_ref, aimb_ref, ldtb_ref,
                      mi_ref, ms_ref, mc_ref, apr_ref, api_ref):
    tc, gl, gc, half = SSM_CHUNK, GROUPS_PER_TILE, S5_GROUP, LANES // 2
    lane = lax.broadcasted_iota(jnp.int32, (gc, LANES), 1)
    lower = lane < half
    sgn = jnp.where(lower, -1.0, 1.0)

    ab_re, ab_im, x_re, x_im = _zoh(are_ref[...], aim_ref[...], ldt_ref[...])
    bc1 = bc1_ref[...]
    bc2 = bc2_ref[...]
    ms_ref[...] = jnp.zeros(ms_ref.shape, ms_ref.dtype)
    w_lags = []
    for d in range(tc):
        w = x_re * bc1 + x_im * (sgn * bc2)
        wsw = x_re * bc2 - x_im * (sgn * bc1)
        w_lags.append(w)
        k = tc - 1 - d
        for g in range(gl):
            keep = lower if g % 2 == 0 else jnp.logical_not(lower)
            re_src, im_src = (w[g], wsw[g]) if g % 2 == 0 else (wsw[g], w[g])
            rows = pl.ds(k * LANES + g * gc, gc)
            ms_ref[rows, pl.ds((g // 2) * LANES, LANES)] = jnp.where(keep, re_src, 0.0).astype(BF16)
            ms_ref[rows, pl.ds((gl // 2 + g // 2) * LANES, LANES)] = (
                jnp.where(keep, im_src, 0.0).astype(BF16))
        x_re, x_im = _cmul(x_re, x_im, ab_re, ab_im)

    pw_re, pw_im = ab_re, ab_im
    for _ in range(tc - 1):
        pw_re, pw_im = _cmul(pw_re, pw_im, ab_re, ab_im)
    apr_ref[...] = pw_re
    api_ref[...] = pw_im

    row = lax.broadcasted_iota(jnp.int32, (LANES, LANES), 0)
    lane_sq = lax.broadcasted_iota(jnp.int32, (LANES, LANES), 1)
    c_sign = jnp.where(row < half, 1.0, -1.0)
    k_rows = []
    for g in range(gl):
        w_stack = jnp.concatenate([w_lags[d][g] for d in range(tc)], axis=0)
        k_g = _dot_3pass(w_stack, cc_ref[g] * c_sign)
        k_rows.append(jnp.where(lane_sq // gc == g, k_g, 0.0))
    zero = jnp.zeros((LANES, LANES), BF16)
    lag_blocks = [jnp.concatenate([k_rows[g][d * gc:(d + 1) * gc] for g in range(gl)],
                                  axis=0).astype(BF16) for d in range(tc)]
    for k in range(tc):
        for kp in range(tc):
            mi_ref[pl.ds(k * LANES, LANES), pl.ds(kp * LANES, LANES)] = (
                lag_blocks[kp - k] if kp >= k else zero)

    ab_re, ab_im, _, _ = _zoh(areb_ref[...], aimb_ref[...], ldtb_ref[...])
    c_re = cc_ref[:, :half, :]
    c_im = cc_ref[:, half:, :]
    grp = lax.broadcasted_iota(jnp.int32, (gl, half, LANES), 0)
    lane3 = lax.broadcasted_iota(jnp.int32, (gl, half, LANES), 2)
    diag = lane3 // gc == grp
    q_re, q_im = ab_re, ab_im
    for kp in range(tc):
        cm_re = jnp.where(diag, c_re * q_re - c_im * q_im, 0.0).astype(BF16)
        cm_im = jnp.where(diag, -(c_re * q_im + c_im * q_re), 0.0).astype(BF16)
        for g in range(gl):
            mc_ref[pl.ds(g * half, half), pl.ds(kp * LANES, LANES)] = cm_re[g]
            mc_ref[pl.ds((gl + g) * half, half), pl.ds(kp * LANES, LANES)] = cm_im[g]
        if kp + 1 < tc:
            q_re, q_im = _cmul(q_re, q_im, ab_re, ab_im)


def _s5_table_inputs(a_re, a_im, log_dt, b_re, b_im, c_re, c_im):
    n_layers, g, p = a_re.shape
    gc = b_re.shape[-1]
    assert 2 * p == LANES and gc == S5_GROUP and g % GROUPS_PER_TILE == 0
    flat = lambda x: x.reshape((n_layers * g,) + x.shape[2:])
    a_re, a_im, log_dt, b_re, b_im, c_re, c_im = map(flat, (a_re, a_im, log_dt, b_re, b_im,
                                                             c_re, c_im))
    are_t = jnp.concatenate([a_re, a_re], axis=-1)[:, None, :]
    aim_t = jnp.concatenate([a_im, a_im], axis=-1)[:, None, :]
    ldt_t = jnp.broadcast_to(log_dt[:, None, None], (n_layers * g, 1, LANES))
    bt_re = jnp.swapaxes(b_re, 1, 2)
    bt_im = jnp.swapaxes(b_im, 1, 2)
    bc1 = jnp.concatenate([bt_re, bt_im], axis=-1)
    bc2 = jnp.concatenate([bt_im, bt_re], axis=-1)
    ct = jnp.concatenate([jnp.swapaxes(c_re, 1, 2), jnp.swapaxes(c_im, 1, 2)], axis=1)
    cc = jnp.tile(ct, (1, 1, GROUPS_PER_TILE))
    return (are_t, aim_t, ldt_t, bc1, bc2, cc, a_re[:, :, None], a_im[:, :, None],
            log_dt[:, None, None])


def _s5_tables(inputs, layer, g):
    p = LANES // 2
    gc = S5_GROUP
    gl = GROUPS_PER_TILE
    nj = g // gl
    kk = SSM_CHUNK * LANES
    ns = 2 * gl * p
    in_blk = lambda r, c: pl.BlockSpec((gl, r, c), lambda j: (layer * nj + j, 0, 0))
    out_blk = pl.BlockSpec((gl, 1, LANES), lambda j: (j, 0, 0))
    mi, ms, mc, apr, api = pl.pallas_call(
        _s5_tables_kernel,
        grid=(nj,),
        in_specs=[in_blk(1, LANES), in_blk(1, LANES), in_blk(1, LANES),
                  in_blk(gc, LANES), in_blk(gc, LANES), in_blk(2 * p, LANES),
                  in_blk(p, 1), in_blk(p, 1), in_blk(1, 1)],
        out_specs=[pl.BlockSpec((None, kk, kk), lambda j: (j, 0, 0)),
                   pl.BlockSpec((None, kk, ns), lambda j: (j, 0, 0)),
                   pl.BlockSpec((None, ns, kk), lambda j: (j, 0, 0)),
                   out_blk, out_blk],
        out_shape=[jax.ShapeDtypeStruct((nj, kk, kk), BF16),
                   jax.ShapeDtypeStruct((nj, kk, ns), BF16),
                   jax.ShapeDtypeStruct((nj, ns, kk), BF16),
                   jax.ShapeDtypeStruct((g, 1, LANES), F32),
                   jax.ShapeDtypeStruct((g, 1, LANES), F32)],
        compiler_params=_params("parallel"),
        name="s5_tables",
    )(*inputs)
    pa_re = apr[:, 0, :p].reshape(1, g * p)
    pa_im = api[:, 0, :p].reshape(1, g * p)
    return mi, ms, mc, pa_re, pa_im


def _s5_in_kernel(x_ref, g_ref, w_ref, cg_ref, co_ref, u_ref, *rest, from_frames):
    k, r = pl.program_id(0), pl.program_id(1)
    nr = pl.num_programs(1)
    step = k * nr + r

    if from_frames:
        hp_ref, cgo_ref, coo_ref, wb_ref, xbuf, sem = rest

        def plane_copy(s, slot):
            return pltpu.make_async_copy(x_ref.at[s % nr, :, s // nr, :], xbuf.at[slot],
                                         sem.at[slot])

        @pl.when(step == 0)
        def _():
            plane_copy(step, 0).start()

        @pl.when(step + 1 < pl.num_programs(0) * nr)
        def _():
            plane_copy(step + 1, (step + 1) % 2).start()

        plane_copy(step, step % 2).wait()
    else:
        cgo_ref, coo_ref, wb_ref = rest

    @pl.when(step == 0)
    def _():
        wb_ref[...] = w_ref[...].astype(BF16)

    if from_frames:
        x = xbuf[step % 2]
        hp_ref[...] = x
        hn = _rms(x, g_ref[...]).astype(BF16)
    else:
        hn = x_ref[...]
    u_ref[...] = _dot(hn, wb_ref[...]).astype(u_ref.dtype)
    cgo_ref[...] = cg_ref[...].astype(BF16)
    coo_ref[...] = co_ref[...].astype(BF16)


def _s5_in(x, g, w, w_glu, w_out, layer, from_frames, tr):
    if from_frames:
        nr, _, tc, d = x.shape
        m = nr * tr
    else:
        tc, m, d = x.shape
        nr = m // tr
    cr = _cast_rows(w_glu, tc * nr)
    tok = pl.BlockSpec((None, tr, d), lambda k, r: (k, r, 0))
    cast_in = pl.BlockSpec((None, cr, d), lambda k, r: (layer, k * nr + r, 0))
    cast_out = pl.BlockSpec((cr, d), lambda k, r: (k * nr + r, 0))
    casts = [jax.ShapeDtypeStruct((d, d), BF16)] * 2
    out_specs = [tok, cast_out, cast_out]
    out_shape = [jax.ShapeDtypeStruct((tc, m, d), BF16)] + casts
    scratch = [pltpu.VMEM((d, d), BF16)]
    if from_frames:
        out_specs.insert(1, tok)
        out_shape.insert(1, jax.ShapeDtypeStruct((tc, m, d), F32))
        scratch += [pltpu.VMEM((2, tr, d), F32), pltpu.SemaphoreType.DMA((2,))]
    res = pl.pallas_call(
        functools.partial(_s5_in_kernel, from_frames=from_frames),
        grid=(tc, nr),
        in_specs=[pl.BlockSpec(memory_space=pl.ANY) if from_frames else tok,
                  pl.BlockSpec((1, d), lambda k, r: (0, 0)),
                  pl.BlockSpec((None, d, d), lambda k, r: (layer, 0, 0), **_RESIDENT),
                  cast_in, cast_in],
        out_specs=out_specs,
        out_shape=out_shape,
        scratch_shapes=scratch,
        compiler_params=_params("arbitrary", "arbitrary"),
        name="s5_in",
    )(x, g, w, w_glu, w_out)
    return res if from_frames else (res[0], None, res[1], res[2])


def _chunk_rows(u_ref):
    return jnp.concatenate([u_ref[k] for k in range(SSM_CHUNK)], axis=-1)


def _s5_state_kernel(u_ref, m_ref, zre_ref, zim_ref):
    z = _dot(_chunk_rows(u_ref), m_ref[...])
    half = z.shape[1] // 2
    zre_ref[...] = z[:, :half]
    zim_ref[...] = z[:, half:]


def _s5_state_in(u, m_state):
    tc, m, d = u.shape
    nj, kk, ns = m_state.shape
    half = ns // 2
    return pl.pallas_call(
        _s5_state_kernel,
        grid=(nj,),
        in_specs=[pl.BlockSpec((tc, m, LANES), lambda j: (0, 0, j)),
                  pl.BlockSpec((None, kk, ns), lambda j: (j, 0, 0))],
        out_specs=[pl.BlockSpec((m, half), lambda j: (0, j)),
                   pl.BlockSpec((m, half), lambda j: (0, j))],
        out_shape=[jax.ShapeDtypeStruct((m, nj * half), F32)] * 2,
        compiler_params=_params("parallel"),
        name="s5_state_in",
    )(u, m_state)


def _s5_scan_kernel(zre_ref, zim_ref, are_ref, aim_ref, sre_ref, sim_ref, st_re, st_im):
    rows = zre_ref.shape[0]

    @pl.when(pl.program_id(1) == 0)
    def _():
        st_re[...] = jnp.zeros_like(st_re)
        st_im[...] = jnp.zeros_like(st_im)

    a_re = are_ref[...]
    a_im = aim_ref[...]

    def step(i, carry):
        s_re, s_im = carry
        row = pl.ds(i, 1)
        sre_ref[row, :] = s_re
        sim_ref[row, :] = s_im
        n_re = a_re * s_re - a_im * s_im + zre_ref[row, :]
        n_im = a_re * s_im + a_im * s_re + zim_ref[row, :]
        return n_re, n_im

    s_re, s_im = lax.fori_loop(0, rows, step, (st_re[...], st_im[...]))
    st_re[...] = s_re
    st_im[...] = s_im


def _s5_scan(z_re, z_im, a_re, a_im, rows_per_seq, tr):
    m, n = z_re.shape
    blk = pl.BlockSpec((tr, n), lambda s, r: (s * (rows_per_seq // tr) + r, 0))
    coef = pl.BlockSpec((1, n), lambda s, r: (0, 0))
    return pl.pallas_call(
        _s5_scan_kernel,
        grid=(m // rows_per_seq, rows_per_seq // tr),
        in_specs=[blk, blk, coef, coef],
        out_specs=[blk, blk],
        out_shape=[jax.ShapeDtypeStruct((m, n), F32)] * 2,
        scratch_shapes=[pltpu.VMEM((1, n), F32), pltpu.VMEM((1, n), F32)],
        compiler_params=_params("arbitrary", "arbitrary"),
        name="s5_scan",
    )(z_re, z_im, a_re, a_im)


def _s5_mix_kernel(u_ref, sre_ref, sim_ref, mi_ref, mc_ref, y_ref):
    x = _chunk_rows(u_ref)
    s = jnp.concatenate([sre_ref[...], sim_ref[...]], axis=-1).astype(BF16)
    kk = mi_ref.shape[0]
    step = 2 * LANES
    for c0 in range(0, kk, step):
        cols = pl.ds(c0, step)
        y = _dot(x[:, :c0 + step], mi_ref[pl.ds(0, c0 + step), cols]) + _dot(s, mc_ref[:, cols])
        y_ref[:, cols] = y.astype(y_ref.dtype)


def _s5_mix(u, s_re, s_im, m_intra, m_carry):
    tc, m, d = u.shape
    nj, kk, _ = m_intra.shape
    ns = m_carry.shape[1]
    half = ns // 2
    return pl.pallas_call(
        _s5_mix_kernel,
        grid=(nj,),
        in_specs=[pl.BlockSpec((tc, m, LANES), lambda j: (0, 0, j)),
                  pl.BlockSpec((m, half), lambda j: (0, j)),
                  pl.BlockSpec((m, half), lambda j: (0, j)),
                  pl.BlockSpec((None, kk, kk), lambda j: (j, 0, 0)),
                  pl.BlockSpec((None, ns, kk), lambda j: (j, 0, 0))],
        out_specs=pl.BlockSpec((None, m, kk), lambda j: (j, 0, 0)),
        out_shape=jax.ShapeDtypeStruct((nj, m, kk), BF16),
        compiler_params=_params("parallel"),
        name="s5_mix",
    )(u, s_re, s_im, m_intra, m_carry)


def _s5_out_kernel(y_ref, u_ref, h_ref, dsk_ref, wg_ref, wo_ref, g_ref, gn_ref, o_ref, hn_ref,
                   *, row_piece):
    nj, tr, _ = y_ref.shape
    for r0 in range(0, tr, row_piece):
        rows = pl.ds(r0, row_piece)
        y = (jnp.concatenate([y_ref[j, rows, :] for j in range(nj)], axis=-1).astype(F32)
             + dsk_ref[...] * u_ref[rows, :].astype(F32))
        z = _gelu(y)
        gate = jax.nn.sigmoid(_dot(z.astype(BF16), wg_ref[...]))
        m = _dot((z * gate).astype(BF16), wo_ref[...])
        _finish(m, h_ref, g_ref, gn_ref, o_ref, hn_ref, rows)


def _s5_out(y, u, h, d_skip, w_glu, w_out, g, g_next, tr):
    nj, m, _ = y.shape
    tc, _, d = h.shape
    tok = pl.BlockSpec((None, tr, d), lambda k, r: (k, r, 0))
    vec = pl.BlockSpec((1, d), lambda k, r: (0, 0))
    mat = pl.BlockSpec((d, d), lambda k, r: (0, 0), **_RESIDENT)
    return pl.pallas_call(
        functools.partial(_s5_out_kernel, row_piece=_tile(tr, 256)),
        grid=(tc, m // tr),
        in_specs=[pl.BlockSpec((nj, tr, LANES), lambda k, r: (0, r, k)),
                  tok, tok, vec, mat, mat, vec, vec],
        out_specs=[tok, tok],
        out_shape=[jax.ShapeDtypeStruct(h.shape, F32), jax.ShapeDtypeStruct(h.shape, BF16)],
        compiler_params=_params("parallel", "parallel"),
        name="s5_out",
    )(y, u, h, d_skip, w_glu, w_out, g, g_next)


def _s5_layer(h, hn, layer, g_pre, g_post, g_next, w_in, table_inputs, groups, d_skip, w_glu,
              w_out, rows_per_seq):
    m_intra, m_state, m_carry, pa_re, pa_im = _s5_tables(table_inputs, layer, groups)
    r = rows_per_seq
    if hn is None:
        u, h, w_glu, w_out = _s5_in(h, g_pre, w_in, w_glu, w_out, layer, True, h.shape[1])
    else:
        u, _, w_glu, w_out = _s5_in(hn, g_pre, w_in, w_glu, w_out, layer, False,
                                    _tile(h.shape[1], 1024))
    z_re, z_im = _s5_state_in(u, m_state)
    s_re, s_im = _s5_scan(z_re, z_im, pa_re, pa_im, r, _tile(r, 64))
    y = _s5_mix(u, s_re, s_im, m_intra, m_carry)
    return _s5_out(y, u, h, d_skip, w_glu, w_out, g_post, g_next, _tile(r, 512))


def _frames_by_offset(w, axes):
    cpb = SGU_BLOCK // SSM_CHUNK
    for ax in axes:
        shape = w.shape
        w = w.reshape(shape[:ax] + (cpb, SSM_CHUNK) + shape[ax + 1:])
        w = jnp.swapaxes(w, ax, ax + 1).reshape(shape)
    return w


def kernel(x, norm_g, s5_w_in, s5_a_re, s5_a_im, s5_log_dt, s5_b_re, s5_b_im, s5_c_re, s5_c_im,
           s5_d, s5_w_glu, s5_w_out, sgu_w_in, sgu_ln_g, sgu_ln_b, sgu_w_s, sgu_b_s, sgu_w_out,
           ffn_w_up, ffn_w_down):
    bsz, seq, d = x.shape
    t = bsz * seq
    tc = SSM_CHUNK
    r = seq // tc
    m = bsz * r
    depth = norm_g.shape[0]
    assert seq % SGU_BLOCK == 0 and d % LANES == 0 and SEQ_CHUNK % tc == 0
    gains = norm_g.reshape(depth, 4, 1, d)
    tr0 = _tile(r, 512)
    h = x.reshape(m // tr0, tr0, tc, d)
    hn = None
    s5_inputs = _s5_table_inputs(s5_a_re, s5_a_im, s5_log_dt, s5_b_re, s5_b_im, s5_c_re, s5_c_im)
    half = sgu_ln_g.shape[-1]
    heads = sgu_w_s.shape[1]
    sgu_w_s = _frames_by_offset(sgu_w_s, (2, 3))
    sgu_b_s = _frames_by_offset(sgu_b_s, (2,)).reshape(-1, heads, SGU_BLOCK, 1)
    for i in range(depth):
        g = gains[i]
        j = i // 2
        if i % 2 == 0:
            h, hn = _s5_layer(h, hn, j, g[0], g[1], g[2], s5_w_in, s5_inputs, s5_a_re.shape[1],
                              s5_d[j].reshape(1, d), s5_w_glu, s5_w_out, r)
        else:
            z, w_out = _up_proj(hn.reshape(t, d), sgu_w_in, sgu_w_out, j, _gelu,
                                _tile(t, 2048), _tile(half, 1024))
            h, hn = _sgu_down(z.reshape(tc, m, 2 * half), sgu_ln_g[j].reshape(1, half),
                              sgu_ln_b[j].reshape(1, half), sgu_w_s[j], sgu_b_s[j],
                              w_out, h, g[1], g[2], _tile(r, 32), _tile(heads, 2))
        a, w_down = _up_proj(hn.reshape(t, d), ffn_w_up, ffn_w_down, i, _relu2,
                             _tile(t, 2048), _tile(ffn_w_up.shape[-1], 1024))
        if i == depth - 1:
            out, _ = _down_proj(a, w_down, h.reshape(t, d), g[3], g[3], (tc, m), _tile(m, 256))
            return out.reshape(bsz, seq, d)
        h, hn = _down_proj(a, w_down, h.reshape(t, d), g[3], gains[i + 1][0], None, _tile(t, 256))
        h = h.reshape(tc, m, d)
        hn = hn.reshape(tc, m, d)
```

```python
import functools
import math

import jax
import jax.numpy as jnp
from jax import lax
from jax.experimental import pallas as pl
from jax.experimental.pallas import tpu as pltpu

F32 = jnp.float32
BF16 = jnp.bfloat16

EPS = 1e-6
EIG_CLIP = -1e-4
SEQ_CHUNK = 64
SGU_BLOCK = 128
S5_GROUP = 16
SSM_CHUNK = 8
LANES = 128
GROUPS_PER_TILE = LANES // S5_GROUP
VMEM_LIMIT = 56 * 1024 * 1024
VMEM_LIMIT_LARGE = 60 * 1024 * 1024

_dot = functools.partial(jnp.dot, preferred_element_type=F32)
_RESIDENT = dict(pipeline_mode=pl.Buffered(1))


def _params(*sem, vmem=VMEM_LIMIT):
    return pltpu.CompilerParams(dimension_semantics=sem, vmem_limit_bytes=vmem)


def _rms(x, g):
    ms = jnp.mean(x * x, axis=-1, keepdims=True)
    return x * lax.rsqrt(ms + EPS) * g


def _relu2(y):
    a = jnp.maximum(y, 0.0)
    return a * a


_GELU_C1 = math.sqrt(2.0 / math.pi)
_GELU_C2 = _GELU_C1 * 0.044715


def _gelu(x):
    hx = 0.5 * x
    return hx + hx * jnp.tanh(x * (_GELU_C1 + _GELU_C2 * (x * x)))


def _tile(n, pref):
    t = min(n, pref)
    assert n % t == 0, (n, t)
    return t


def _up_kernel(x_ref, w_ref, cw_ref, o_ref, cwo_ref, wb_ref, *, act, row_pieces):
    @pl.when(pl.program_id(1) == 0)
    def _():
        wb_ref[...] = w_ref[...].astype(BF16)

    r0 = 0
    for size in row_pieces:
        rows = pl.ds(r0, size)
        o_ref[rows, :] = act(_dot(x_ref[rows, :], wb_ref[...])).astype(o_ref.dtype)
        r0 += size
    cwo_ref[...] = cw_ref[...].astype(BF16)


def _cast_rows(cast_w, steps):
    rows = cast_w.shape[1] // steps
    assert rows * steps == cast_w.shape[1] and rows % 16 == 0
    return rows


def _up_proj(hn, w, cast_w, layer, act, tm, tn):
    t, d = hn.shape
    n = w.shape[2]
    nm = t // tm
    cr = _cast_rows(cast_w, (n // tn) * nm)
    cc = cast_w.shape[2]
    assert tm % LANES == 0
    return pl.pallas_call(
        functools.partial(_up_kernel, act=act, row_pieces=(tm // 2, 3 * tm // 8, tm // 8)),
        grid=(n // tn, nm),
        in_specs=[pl.BlockSpec((tm, d), lambda j, i: (i, 0)),
                  pl.BlockSpec((None, d, tn), lambda j, i: (layer, 0, j)),
                  pl.BlockSpec((None, cr, cc), lambda j, i: (layer, j * nm + i, 0))],
        out_specs=[pl.BlockSpec((tm, tn), lambda j, i: (i, j)),
                   pl.BlockSpec((cr, cc), lambda j, i: (j * nm + i, 0))],
        out_shape=[jax.ShapeDtypeStruct((t, n), BF16),
                   jax.ShapeDtypeStruct(cast_w.shape[1:], BF16)],
        scratch_shapes=[pltpu.VMEM((d, tn), BF16)],
        compiler_params=_params("arbitrary", "arbitrary", vmem=VMEM_LIMIT_LARGE),
        name="up_proj",
    )(hn, w, cast_w)


def _finish(f, h_ref, g_ref, gn_ref, o_ref, hn_ref, rows=None):
    if rows is None:
        h = h_ref[...].reshape(f.shape) + _rms(f, g_ref[...])
        o_ref[...] = h.reshape(o_ref.shape)
        if hn_ref is not None:
            hn_ref[...] = _rms(h, gn_ref[...]).astype(BF16).reshape(hn_ref.shape)
    else:
        h = h_ref[rows, :] + _rms(f, g_ref[...])
        o_ref[rows, :] = h
        if hn_ref is not None:
            hn_ref[rows, :] = _rms(h, gn_ref[...]).astype(BF16)


def _down_kernel(a_ref, w_ref, h_ref, g_ref, gn_ref, o_ref, *rest, to_frames):
    f = _dot(a_ref[...], w_ref[...])
    if not to_frames:
        _finish(f, h_ref, g_ref, gn_ref, o_ref, rest[0])
        return
    obuf, sem = rest
    i = pl.program_id(0)
    n = pl.num_programs(0)
    row_tiles = o_ref.shape[0]

    def plane_copy(s, slot):
        return pltpu.make_async_copy(obuf.at[slot], o_ref.at[s % row_tiles, :, s // row_tiles, :],
                                     sem.at[slot])

    slot = i % 2

    @pl.when(i >= 2)
    def _():
        plane_copy(i - 2, slot).wait()

    _finish(f, h_ref, g_ref, gn_ref, obuf.at[slot], None)
    plane_copy(i, slot).start()

    @pl.when(i == n - 1)
    def _():
        plane_copy(i - 1, 1 - slot).wait()
        plane_copy(i, slot).wait()


def _down_proj(a, w, h, g, g_next, frames, tm):
    t, kk = a.shape
    d = w.shape[1]
    n_tiles = t // tm
    tok = pl.BlockSpec((tm, d), lambda i: (i, 0))
    vec = pl.BlockSpec((1, d), lambda i: (0, 0))
    if frames is None:
        out_specs = [tok, tok]
        out_shape = [jax.ShapeDtypeStruct((t, d), F32), jax.ShapeDtypeStruct((t, d), BF16)]
        scratch = []
    else:
        planes, rows = frames
        assert planes * rows == t and rows % tm == 0 and n_tiles >= 2
        out_specs = [pl.BlockSpec(memory_space=pl.ANY)]
        out_shape = [jax.ShapeDtypeStruct((rows // tm, tm, planes, d), F32)]
        scratch = [pltpu.VMEM((2, tm, d), F32), pltpu.SemaphoreType.DMA((2,))]
    res = pl.pallas_call(
        functools.partial(_down_kernel, to_frames=frames is not None),
        grid=(n_tiles,),
        in_specs=[pl.BlockSpec((tm, kk), lambda i: (i, 0)),
                  pl.BlockSpec((kk, d), lambda i: (0, 0), **_RESIDENT),
                  tok, vec, vec],
        out_specs=out_specs,
        out_shape=out_shape,
        scratch_shapes=scratch,
        compiler_params=_params("arbitrary", vmem=VMEM_LIMIT_LARGE),
        name="down_proj",
    )(a, w, h, g, g_next)
    return res if frames is None else (res[0], None)


def _sgu_down_kernel(u_ref, v_ref, lg_ref, lb_ref, ws_ref, bs_ref, w_ref,
                     h_ref, g_ref, gn_ref, o_ref, hn_ref, gate_ref, *, heads_per_group):
    tc, mt, half = u_ref.shape
    heads = ws_ref.shape[0]
    hd = half // heads
    cpb = SGU_BLOCK // tc
    cps = SEQ_CHUNK // tc

    row = lax.broadcasted_iota(jnp.int32, (SGU_BLOCK, SGU_BLOCK), 0)
    col = lax.broadcasted_iota(jnp.int32, (SGU_BLOCK, SGU_BLOCK), 1)
    causal = (col % cpb) // cps <= (row % cpb) // cps
    s1 = jnp.zeros((tc, mt, LANES), F32)
    s2 = jnp.zeros((tc, mt, LANES), F32)
    for c in range(half // LANES):
        vc = v_ref[:, :, pl.ds(c * LANES, LANES)].astype(F32)
        s1 += vc
        s2 += vc * vc
    mean = jnp.sum(s1, axis=-1, keepdims=True) * (1.0 / half)
    rstd = lax.rsqrt(jnp.sum(s2, axis=-1, keepdims=True) * (1.0 / half) - mean * mean + EPS)

    def block_of(x, n):
        return jnp.concatenate([x[o, n * cpb:(n + 1) * cpb] for o in range(tc)], axis=0)

    f = None
    for hh in range(heads):
        cols = pl.ds(hh * hd, hd)
        w = jnp.where(causal, ws_ref[hh], 0.0).astype(BF16)
        lg = lg_ref[:, cols]
        lb = lb_ref[:, cols]
        bias = bs_ref[hh]
        for n in range(mt // cpb):
            rows = pl.ds(n * cpb, cpb)
            v = jnp.concatenate([v_ref[o, rows, cols] for o in range(tc)], axis=0).astype(F32)
            vn = ((v - block_of(mean, n)) * block_of(rstd, n) * lg + lb).astype(BF16)
            s = _dot(w, vn) + bias
            u = jnp.concatenate([u_ref[o, rows, cols] for o in range(tc)], axis=0).astype(F32)
            gated = (u * s).astype(BF16)
            for o in range(tc):
                gate_ref[o, rows, cols] = gated[o * cpb:(o + 1) * cpb]
        if (hh + 1) % heads_per_group == 0:
            gcols = pl.ds((hh + 1 - heads_per_group) * hd, heads_per_group * hd)
            part = _dot(gate_ref[:, :, gcols].reshape(tc * mt, heads_per_group * hd),
                        w_ref[gcols, :])
            f = part if f is None else f + part
    _finish(f, h_ref, g_ref, gn_ref, o_ref, hn_ref)


def _sgu_down(z, ln_g, ln_b, w_s, b_s, w_out, h, g, g_next, mt, heads_per_group):
    tc, m, n = z.shape
    half = n // 2
    heads = w_s.shape[0]
    assert heads % heads_per_group == 0
    d = w_out.shape[1]
    tok = lambda c: pl.BlockSpec((tc, mt, c), lambda i: (0, i, 0))
    const = lambda *shape: pl.BlockSpec(shape, lambda i: (0,) * len(shape), **_RESIDENT)
    return pl.pallas_call(
        functools.partial(_sgu_down_kernel, heads_per_group=heads_per_group),
        grid=(m // mt,),
        in_specs=[pl.BlockSpec((tc, mt, half), lambda i: (0, i, 0)),
                  pl.BlockSpec((tc, mt, half), lambda i: (0, i, 1)),
                  const(1, half), const(1, half),
                  const(heads, SGU_BLOCK, SGU_BLOCK), const(heads, SGU_BLOCK, 1),
                  const(half, d),
                  tok(d), const(1, d), const(1, d)],
        out_specs=[tok(d), tok(d)],
        out_shape=[jax.ShapeDtypeStruct((tc, m, d), F32), jax.ShapeDtypeStruct((tc, m, d), BF16)],
        scratch_shapes=[pltpu.VMEM((tc, mt, half), BF16)],
        compiler_params=_params("parallel", vmem=VMEM_LIMIT_LARGE),
        name="sgu_down",
    )(z, z, ln_g, ln_b, w_s, b_s, w_out, h, g, g_next)


def _zoh(a_re, a_im, log_dt):
    lam_re = jnp.minimum(a_re, EIG_CLIP)
    lam_im = a_im
    dt = jnp.exp(log_dt)
    mag = jnp.exp(lam_re * dt)
    ab_re = mag * jnp.cos(lam_im * dt)
    ab_im = mag * jnp.sin(lam_im * dt)
    denom = lam_re * lam_re + lam_im * lam_im
    coef_re = ((ab_re - 1.0) * lam_re + ab_im * lam_im) / denom
    coef_im = (ab_im * lam_re - (ab_re - 1.0) * lam_im) / denom
    return ab_re, ab_im, coef_re, coef_im


def _cmul(x_re, x_im, y_re, y_im):
    return x_re * y_re - x_im * y_im, x_re * y_im + x_im * y_re


def _dot_3pass(a, b):
    a_hi = a.astype(BF16)
    a_lo = (a - a_hi.astype(F32)).astype(BF16)
    b_hi = b.astype(BF16)
    b_lo = (b - b_hi.astype(F32)).astype(BF16)
    return _dot(a_hi, b_hi) + _dot(a_hi, b_lo) + _dot(a_lo, b_hi)


def _s5_tables_kernel(are_ref, aim_ref, ldt_ref, bc1_ref, bc2_ref, cc_ref,
                      areb_ref, aimb_ref, ldtb_ref,
                      mi_ref, ms_ref, mc_ref, apr_ref, api_ref):
    tc, gl, gc, half = SSM_CHUNK, GROUPS_PER_TILE, S5_GROUP, LANES // 2
    lane = lax.broadcasted_iota(jnp.int32, (gc, LANES), 1)
    lower = lane < half
    sgn = jnp.where(lower, -1.0, 1.0)

    ab_re, ab_im, x_re, x_im = _zoh(are_ref[...], aim_ref[...], ldt_ref[...])
    bc1 = bc1_ref[...]
    bc2 = bc2_ref[...]
    ms_ref[...] = jnp.zeros(ms_ref.shape, ms_ref.dtype)
    w_lags = []
    for d in range(tc):
        w = x_re * bc1 + x_im * (sgn * bc2)
        wsw = x_re * bc2 - x_im * (sgn * bc1)
        w_lags.append(w)
        k = tc - 1 - d
        for g in range(gl):
            keep = lower if g % 2 == 0 else jnp.logical_not(lower)
            re_src, im_src = (w[g], wsw[g]) if g % 2 == 0 else (wsw[g], w[g])
            rows = pl.ds(k * LANES + g * gc, gc)
            ms_ref[rows, pl.ds((g // 2) * LANES, LANES)] = jnp.where(keep, re_src, 0.0).astype(BF16)
            ms_ref[rows, pl.ds((gl // 2 + g // 2) * LANES, LANES)] = (
                jnp.where(keep, im_src, 0.0).astype(BF16))
        x_re, x_im = _cmul(x_re, x_im, ab_re, ab_im)

    pw_re, pw_im = ab_re, ab_im
    for _ in range(tc - 1):
        pw_re, pw_im = _cmul(pw_re, pw_im, ab_re, ab_im)
    apr_ref[...] = pw_re
    api_ref[...] = pw_im

    row = lax.broadcasted_iota(jnp.int32, (LANES, LANES), 0)
    lane_sq = lax.broadcasted_iota(jnp.int32, (LANES, LANES), 1)
    c_sign = jnp.where(row < half, 1.0, -1.0)
    k_rows = []
    for g in range(gl):
        w_stack = jnp.concatenate([w_lags[d][g] for d in range(tc)], axis=0)
        k_g = _dot_3pass(w_stack, cc_ref[g] * c_sign)
        k_rows.append(jnp.where(lane_sq // gc == g, k_g, 0.0))
    zero = jnp.zeros((LANES, LANES), BF16)
    lag_blocks = [jnp.concatenate([k_rows[g][d * gc:(d + 1) * gc] for g in range(gl)],
                                  axis=0).astype(BF16) for d in range(tc)]
    for k in range(tc):
        for kp in range(tc):
            mi_ref[pl.ds(k * LANES, LANES), pl.ds(kp * LANES, LANES)] = (
                lag_blocks[kp - k] if kp >= k else zero)

    ab_re, ab_im, _, _ = _zoh(areb_ref[...], aimb_ref[...], ldtb_ref[...])
    c_re = cc_ref[:, :half, :]
    c_im = cc_ref[:, half:, :]
    grp = lax.broadcasted_iota(jnp.int32, (gl, half, LANES), 0)
    lane3 = lax.broadcasted_iota(jnp.int32, (gl, half, LANES), 2)
    diag = lane3 // gc == grp
    q_re, q_im = ab_re, ab_im
    for kp in range(tc):
        cm_re = jnp.where(diag, c_re * q_re - c_im * q_im, 0.0).astype(BF16)
        cm_im = jnp.where(diag, -(c_re * q_im + c_im * q_re), 0.0).astype(BF16)
        for g in range(gl):
            mc_ref[pl.ds(g * half, half), pl.ds(kp * LANES, LANES)] = cm_re[g]
            mc_ref[pl.ds((gl + g) * half, half), pl.ds(kp * LANES, LANES)] = cm_im[g]
        if kp + 1 < tc:
            q_re, q_im = _cmul(q_re, q_im, ab_re, ab_im)


def _s5_table_inputs(a_re, a_im, log_dt, b_re, b_im, c_re, c_im):
    n_layers, g, p = a_re.shape
    gc = b_re.shape[-1]
    assert 2 * p == LANES and gc == S5_GROUP and g % GROUPS_PER_TILE == 0
    flat = lambda x: x.reshape((n_layers * g,) + x.shape[2:])
    a_re, a_im, log_dt, b_re, b_im, c_re, c_im = map(flat, (a_re, a_im, log_dt, b_re, b_im,
                                                             c_re, c_im))
    are_t = jnp.concatenate([a_re, a_re], axis=-1)[:, None, :]
    aim_t = jnp.concatenate([a_im, a_im], axis=-1)[:, None, :]
    ldt_t = jnp.broadcast_to(log_dt[:, None, None], (n_layers * g, 1, LANES))
    bt_re = jnp.swapaxes(b_re, 1, 2)
    bt_im = jnp.swapaxes(b_im, 1, 2)
    bc1 = jnp.concatenate([bt_re, bt_im], axis=-1)
    bc2 = jnp.concatenate([bt_im, bt_re], axis=-1)
    ct = jnp.concatenate([jnp.swapaxes(c_re, 1, 2), jnp.swapaxes(c_im, 1, 2)], axis=1)
    cc = jnp.tile(ct, (1, 1, GROUPS_PER_TILE))
    return (are_t, aim_t, ldt_t, bc1, bc2, cc, a_re[:, :, None], a_im[:, :, None],
            log_dt[:, None, None])


def _s5_tables(inputs, layer, g):
    p = LANES // 2
    gc = S5_GROUP
    gl = GROUPS_PER_TILE
    nj = g // gl
    kk = SSM_CHUNK * LANES
    ns = 2 * gl * p
    in_blk = lambda r, c: pl.BlockSpec((gl, r, c), lambda j: (layer * nj + j, 0, 0))
    out_blk = pl.BlockSpec((gl, 1, LANES), lambda j: (j, 0, 0))
    mi, ms, mc, apr, api = pl.pallas_call(
        _s5_tables_kernel,
        grid=(nj,),
        in_specs=[in_blk(1, LANES), in_blk(1, LANES), in_blk(1, LANES),
                  in_blk(gc, LANES), in_blk(gc, LANES), in_blk(2 * p, LANES),
                  in_blk(p, 1), in_blk(p, 1), in_blk(1, 1)],
        out_specs=[pl.BlockSpec((None, kk, kk), lambda j: (j, 0, 0)),
                   pl.BlockSpec((None, kk, ns), lambda j: (j, 0, 0)),
                   pl.BlockSpec((None, ns, kk), lambda j: (j, 0, 0)),
                   out_blk, out_blk],
        out_shape=[jax.ShapeDtypeStruct((nj, kk, kk), BF16),
                   jax.ShapeDtypeStruct((nj, kk, ns), BF16),
                   jax.ShapeDtypeStruct((nj, ns, kk), BF16),
                   jax.ShapeDtypeStruct((g, 1, LANES), F32),
                   jax.ShapeDtypeStruct((g, 1, LANES), F32)],
        compiler_params=_params("parallel"),
        name="s5_tables",
    )(*inputs)
    pa_re = apr[:, 0, :p].reshape(1, g * p)
    pa_im = api[:, 0, :p].reshape(1, g * p)
    return mi, ms, mc, pa_re, pa_im


def _s5_in_kernel(x_ref, g_ref, w_ref, cg_ref, co_ref, u_ref, *rest, from_frames):
    k, r = pl.program_id(0), pl.program_id(1)
    nr = pl.num_programs(1)
    step = k * nr + r

    if from_frames:
        hp_ref, cgo_ref, coo_ref, wb_ref, xbuf, sem = rest

        def plane_copy(s, slot):
            return pltpu.make_async_copy(x_ref.at[s % nr, :, s // nr, :], xbuf.at[slot],
                                         sem.at[slot])

        @pl.when(step == 0)
        def _():
            plane_copy(step, 0).start()

        @pl.when(step + 1 < pl.num_programs(0) * nr)
        def _():
            plane_copy(step + 1, (step + 1) % 2).start()

        plane_copy(step, step % 2).wait()
    else:
        cgo_ref, coo_ref, wb_ref = rest

    @pl.when(step == 0)
    def _():
        wb_ref[...] = w_ref[...].astype(BF16)

    if from_frames:
        x = xbuf[step % 2]
        hp_ref[...] = x
        hn = _rms(x, g_ref[...]).astype(BF16)
    else:
        hn = x_ref[...]
    u_ref[...] = _dot(hn, wb_ref[...]).astype(u_ref.dtype)
    cgo_ref[...] = cg_ref[...].astype(BF16)
    coo_ref[...] = co_ref[...].astype(BF16)


def _s5_in(x, g, w, w_glu, w_out, layer, from_frames, tr):
    if from_frames:
        nr, _, tc, d = x.shape
        m = nr * tr
    else:
        tc, m, d = x.shape
        nr = m // tr
    cr = _cast_rows(w_glu, tc * nr)
    tok = pl.BlockSpec((None, tr, d), lambda k, r: (k, r, 0))
    cast_in = pl.BlockSpec((None, cr, d), lambda k, r: (layer, k * nr + r, 0))
    cast_out = pl.BlockSpec((cr, d), lambda k, r: (k * nr + r, 0))
    casts = [jax.ShapeDtypeStruct((d, d), BF16)] * 2
    out_specs = [tok, cast_out, cast_out]
    out_shape = [jax.ShapeDtypeStruct((tc, m, d), BF16)] + casts
    scratch = [pltpu.VMEM((d, d), BF16)]
    if from_frames:
        out_specs.insert(1, tok)
        out_shape.insert(1, jax.ShapeDtypeStruct((tc, m, d), F32))
        scratch += [pltpu.VMEM((2, tr, d), F32), pltpu.SemaphoreType.DMA((2,))]
    res = pl.pallas_call(
        functools.partial(_s5_in_kernel, from_frames=from_frames),
        grid=(tc, nr),
        in_specs=[pl.BlockSpec(memory_space=pl.ANY) if from_frames else tok,
                  pl.BlockSpec((1, d), lambda k, r: (0, 0)),
                  pl.BlockSpec((None, d, d), lambda k, r: (layer, 0, 0), **_RESIDENT),
                  cast_in, cast_in],
        out_specs=out_specs,
        out_shape=out_shape,
        scratch_shapes=scratch,
        compiler_params=_params("arbitrary", "arbitrary"),
        name="s5_in",
    )(x, g, w, w_glu, w_out)
    return res if from_frames else (res[0], None, res[1], res[2])


def _chunk_rows(u_ref):
    return jnp.concatenate([u_ref[k] for k in range(SSM_CHUNK)], axis=-1)


def _s5_state_kernel(u_ref, m_ref, zre_ref, zim_ref):
    z = _dot(_chunk_rows(u_ref), m_ref[...])
    half = z.shape[1] // 2
    zre_ref[...] = z[:, :half]
    zim_ref[...] = z[:, half:]


def _s5_state_in(u, m_state):
    tc, m, d = u.shape
    nj, kk, ns = m_state.shape
    half = ns // 2
    return pl.pallas_call(
        _s5_state_kernel,
        grid=(nj,),
        in_specs=[pl.BlockSpec((tc, m, LANES), lambda j: (0, 0, j)),
                  pl.BlockSpec((None, kk, ns), lambda j: (j, 0, 0))],
        out_specs=[pl.BlockSpec((m, half), lambda j: (0, j)),
                   pl.BlockSpec((m, half), lambda j: (0, j))],
        out_shape=[jax.ShapeDtypeStruct((m, nj * half), F32)] * 2,
        compiler_params=_params("parallel"),
        name="s5_state_in",
    )(u, m_state)


def _s5_scan_kernel(zre_ref, zim_ref, are_ref, aim_ref, sre_ref, sim_ref, st_re, st_im):
    rows = zre_ref.shape[0]

    @pl.when(pl.program_id(1) == 0)
    def _():
        st_re[...] = jnp.zeros_like(st_re)
        st_im[...] = jnp.zeros_like(st_im)

    a_re = are_ref[...]
    a_im = aim_ref[...]

    def step(i, carry):
        s_re, s_im = carry
        row = pl.ds(i, 1)
        sre_ref[row, :] = s_re
        sim_ref[row, :] = s_im
        n_re = a_re * s_re - a_im * s_im + zre_ref[row, :]
        n_im = a_re * s_im + a_im * s_re + zim_ref[row, :]
        return n_re, n_im

    s_re, s_im = lax.fori_loop(0, rows, step, (st_re[...], st_im[...]))
    st_re[...] = s_re
    st_im[...] = s_im


def _s5_scan(z_re, z_im, a_re, a_im, rows_per_seq, tr):
    m, n = z_re.shape
    blk = pl.BlockSpec((tr, n), lambda s, r: (s * (rows_per_seq // tr) + r, 0))
    coef = pl.BlockSpec((1, n), lambda s, r: (0, 0))
    return pl.pallas_call(
        _s5_scan_kernel,
        grid=(m // rows_per_seq, rows_per_seq // tr),
        in_specs=[blk, blk, coef, coef],
        out_specs=[blk, blk],
        out_shape=[jax.ShapeDtypeStruct((m, n), F32)] * 2,
        scratch_shapes=[pltpu.VMEM((1, n), F32), pltpu.VMEM((1, n), F32)],
        compiler_params=_params("arbitrary", "arbitrary"),
        name="s5_scan",
    )(z_re, z_im, a_re, a_im)


def _s5_mix_kernel(u_ref, sre_ref, sim_ref, mi_ref, mc_ref, y_ref):
    x = _chunk_rows(u_ref)
    s = jnp.concatenate([sre_ref[...], sim_ref[...]], axis=-1).astype(BF16)
    kk = mi_ref.shape[0]
    step = 2 * LANES
    for c0 in range(0, kk, step):
        cols = pl.ds(c0, step)
        y = _dot(x[:, :c0 + step], mi_ref[pl.ds(0, c0 + step), cols]) + _dot(s, mc_ref[:, cols])
        y_ref[:, cols] = y.astype(y_ref.dtype)


def _s5_mix(u, s_re, s_im, m_intra, m_carry):
    tc, m, d = u.shape
    nj, kk, _ = m_intra.shape
    ns = m_carry.shape[1]
    half = ns // 2
    return pl.pallas_call(
        _s5_mix_kernel,
        grid=(nj,),
        in_specs=[pl.BlockSpec((tc, m, LANES), lambda j: (0, 0, j)),
                  pl.BlockSpec((m, half), lambda j: (0, j)),
                  pl.BlockSpec((m, half), lambda j: (0, j)),
                  pl.BlockSpec((None, kk, kk), lambda j: (j, 0, 0)),
                  pl.BlockSpec((None, ns, kk), lambda j: (j, 0, 0))],
        out_specs=pl.BlockSpec((None, m, kk), lambda j: (j, 0, 0)),
        out_shape=jax.ShapeDtypeStruct((nj, m, kk), BF16),
        compiler_params=_params("parallel"),
        name="s5_mix",
    )(u, s_re, s_im, m_intra, m_carry)


def _s5_out_kernel(y_ref, u_ref, h_ref, dsk_ref, wg_ref, wo_ref, g_ref, gn_ref, o_ref, hn_ref,
                   *, row_piece):
    nj, tr, _ = y_ref.shape
    for r0 in range(0, tr, row_piece):
        rows = pl.ds(r0, row_piece)
        y = (jnp.concatenate([y_ref[j, rows, :] for j in range(nj)], axis=-1).astype(F32)
             + dsk_ref[...] * u_ref[rows, :].astype(F32))
        z = _gelu(y)
        gate = jax.nn.sigmoid(_dot(z.astype(BF16), wg_ref[...]))
        m = _dot((z * gate).astype(BF16), wo_ref[...])
        _finish(m, h_ref, g_ref, gn_ref, o_ref, hn_ref, rows)


def _s5_out(y, u, h, d_skip, w_glu, w_out, g, g_next, tr):
    nj, m, _ = y.shape
    tc, _, d = h.shape
    tok = pl.BlockSpec((None, tr, d), lambda k, r: (k, r, 0))
    vec = pl.BlockSpec((1, d), lambda k, r: (0, 0))
    mat = pl.BlockSpec((d, d), lambda k, r: (0, 0), **_RESIDENT)
    return pl.pallas_call(
        functools.partial(_s5_out_kernel, row_piece=_tile(tr, 256)),
        grid=(tc, m // tr),
        in_specs=[pl.BlockSpec((nj, tr, LANES), lambda k, r: (0, r, k)),
                  tok, tok, vec, mat, mat, vec, vec],
        out_specs=[tok, tok],
        out_shape=[jax.ShapeDtypeStruct(h.shape, F32), jax.ShapeDtypeStruct(h.shape, BF16)],
        compiler_params=_params("parallel", "parallel"),
        name="s5_out",
    )(y, u, h, d_skip, w_glu, w_out, g, g_next)


def _s5_layer(h, hn, layer, g_pre, g_post, g_next, w_in, table_inputs, groups, d_skip, w_glu,
              w_out, rows_per_seq):
    m_intra, m_state, m_carry, pa_re, pa_im = _s5_tables(table_inputs, layer, groups)
    r = rows_per_seq
    if hn is None:
        u, h, w_glu, w_out = _s5_in(h, g_pre, w_in, w_glu, w_out, layer, True, h.shape[1])
    else:
        u, _, w_glu, w_out = _s5_in(hn, g_pre, w_in, w_glu, w_out, layer, False,
                                    _tile(h.shape[1], 1024))
    z_re, z_im = _s5_state_in(u, m_state)
    s_re, s_im = _s5_scan(z_re, z_im, pa_re, pa_im, r, _tile(r, 64))
    y = _s5_mix(u, s_re, s_im, m_intra, m_carry)
    return _s5_out(y, u, h, d_skip, w_glu, w_out, g_post, g_next, _tile(r, 512))


def _frames_by_offset(w, axes):
    cpb = SGU_BLOCK // SSM_CHUNK
    for ax in axes:
        shape = w.shape
        w = w.reshape(shape[:ax] + (cpb, SSM_CHUNK) + shape[ax + 1:])
        w = jnp.swapaxes(w, ax, ax + 1).reshape(shape)
    return w


def kernel(x, norm_g, s5_w_in, s5_a_re, s5_a_im, s5_log_dt, s5_b_re, s5_b_im, s5_c_re, s5_c_im,
           s5_d, s5_w_glu, s5_w_out, sgu_w_in, sgu_ln_g, sgu_ln_b, sgu_w_s, sgu_b_s, sgu_w_out,
           ffn_w_up, ffn_w_down):
    bsz, seq, d = x.shape
    t = bsz * seq
    tc = SSM_CHUNK
    r = seq // tc
    m = bsz * r
    depth = norm_g.shape[0]
    assert seq % SGU_BLOCK == 0 and d % LANES == 0 and SEQ_CHUNK % tc == 0
    gains = norm_g.reshape(depth, 4, 1, d)
    tr0 = _tile(r, 512)
    h = x.reshape(m // tr0, tr0, tc, d)
    hn = None
    s5_inputs = _s5_table_inputs(s5_a_re, s5_a_im, s5_log_dt, s5_b_re, s5_b_im, s5_c_re, s5_c_im)
    half = sgu_ln_g.shape[-1]
    heads = sgu_w_s.shape[1]
    sgu_w_s = _frames_by_offset(sgu_w_s, (2, 3))
    sgu_b_s = _frames_by_offset(sgu_b_s, (2,)).reshape(-1, heads, SGU_BLOCK, 1)
    for i in range(depth):
        g = gains[i]
        j = i // 2
        if i % 2 == 0:
            h, hn = _s5_layer(h, hn, j, g[0], g[1], g[2], s5_w_in, s5_inputs, s5_a_re.shape[1],
                              s5_d[j].reshape(1, d), s5_w_glu, s5_w_out, r)
        else:
            z, w_out = _up_proj(hn.reshape(t, d), sgu_w_in, sgu_w_out, j, _gelu,
                                _tile(t, 2048), _tile(half, 1024))
            h, hn = _sgu_down(z.reshape(tc, m, 2 * half), sgu_ln_g[j].reshape(1, half),
                              sgu_ln_b[j].reshape(1, half), sgu_w_s[j], sgu_b_s[j],
                              w_out, h, g[1], g[2], _tile(r, 32), _tile(heads, 2))
        a, w_down = _up_proj(hn.reshape(t, d), ffn_w_up, ffn_w_down, i, _relu2,
                             _tile(t, 2048), _tile(ffn_w_up.shape[-1], 1024))
        if i == depth - 1:
            out, _ = _down_proj(a, w_down, h.reshape(t, d), g[3], g[3], (tc, m), _tile(m, 256))
            return out.reshape(bsz, seq, d)
        h, hn = _down_proj(a, w_down, h.reshape(t, d), g[3], gains[i + 1][0], None, _tile(t, 256))
        h = h.reshape(tc, m, d)
        hn = hn.reshape(tc, m, d)
```
